```python
import jax, jax.numpy as jnp
from jax import lax
import numpy as np

D_MODEL = 1024
BATCH = 2
SEQ = 16384
DEPTH = 2

HGRN_WIDTH = D_MODEL // 2
HGRN_EXPAND = 128
HGRN_HEADS = HGRN_WIDTH // HGRN_EXPAND
HGRN_DK = HGRN_EXPAND
HGRN_DV = HGRN_WIDTH // HGRN_HEADS
HGRN_CHUNK = 64
ATTN_WIDTH = D_MODEL - HGRN_WIDTH
ATTN_HEADS = 8
ATTN_DH = ATTN_WIDTH // ATTN_HEADS
DILATED_PATTERNS = ((128, 1), (512, 4), (2048, 16))
D_FF = 4 * D_MODEL
IN_SIZES = (HGRN_WIDTH, HGRN_WIDTH, HGRN_WIDTH, HGRN_WIDTH, ATTN_WIDTH, ATTN_WIDTH, ATTN_WIDTH)
IN_COLS = sum(IN_SIZES)
IN_SPLITS = tuple(int(c) for c in np.cumsum(IN_SIZES)[:-1])
DEEPNORM_ALPHA = (2 * DEPTH) ** 0.25
DEEPNORM_BETA = (8 * DEPTH) ** -0.25
LN_EPS = 1e-5
RMS_EPS = 1e-6

kernel_name = "hymba_style_hgrn2_dilated_attn_deepnorm"


def layer_norm(x, g, b):
    xf = x.astype(jnp.float32)
    mu = jnp.mean(xf, axis=-1, keepdims=True)
    var = jnp.mean(jnp.square(xf - mu), axis=-1, keepdims=True)
    y = (xf - mu) * lax.rsqrt(var + LN_EPS) * g.astype(jnp.float32) + b.astype(jnp.float32)
    return y.astype(x.dtype)


def hgrn2_mixer(q, f_logit, i, gate, lb, norm_w):
    B, S, _ = q.shape
    H, dk, dv, C = HGRN_HEADS, HGRN_DK, HGRN_DV, HGRN_CHUNK
    nc = S // C
    f32 = jnp.float32
    q = jax.nn.silu(q.astype(f32))
    lb = lb.astype(f32)
    log_f = jnp.logaddexp(jnp.log(lb), jnp.log1p(-lb) + jax.nn.log_sigmoid(f_logit.astype(f32)))
    k = -jnp.expm1(log_f)
    v = i.astype(f32)

    def to_chunks(t, d):
        return t.reshape(B, nc, C, H, d).transpose(1, 0, 3, 2, 4)

    tri = jnp.tril(jnp.ones((C, C), dtype=bool))[:, :, None]

    def step(state, inp):
        qc, kc, vc, gc = inp
        G = jnp.cumsum(gc, axis=2)
        o_inter = jnp.einsum('bhtk,bhkv->bhtv', qc * jnp.exp(G), state)
        diff = G[:, :, :, None, :] - G[:, :, None, :, :]
        decay = jnp.exp(jnp.where(tri, diff, -jnp.inf))
        scores = jnp.einsum('bhtk,bhsk,bhtsk->bhts', qc, kc, decay)
        o_intra = jnp.einsum('bhts,bhsv->bhtv', scores, vc)
        G_last = G[:, :, -1:, :]
        k_dec = kc * jnp.exp(G_last - G)
        new_state = jnp.exp(G_last[:, :, 0, :])[..., None] * state + jnp.einsum('bhsk,bhsv->bhkv', k_dec, vc)
        return new_state, o_inter + o_intra

    state0 = jnp.zeros((B, H, dk, dv), f32)
    _, o = lax.scan(step, state0, (to_chunks(q, dk), to_chunks(k, dk), to_chunks(v, dv), to_chunks(log_f, dk)))
    o = o.transpose(1, 0, 3, 2, 4).reshape(B, S, H, dv)
    o = o * lax.rsqrt(jnp.mean(o * o, axis=-1, keepdims=True) + RMS_EPS) * norm_w.astype(f32).reshape(H, dv)
    return o.reshape(B, S, H * dv) * jax.nn.sigmoid(gate.astype(f32))


def dilated_branch(q, k, v, window, dilation):
    B, S, H, dh = q.shape
    L = S // dilation
    blk = window // dilation
    nb = -(-L // blk)
    Lp = nb * blk

    def sub(t):
        t = t.reshape(B, L, dilation, H, dh).transpose(0, 3, 2, 1, 4)
        t = jnp.pad(t, ((0, 0), (0, 0), (0, 0), (0, Lp - L), (0, 0)))
        return t.reshape(B, H, dilation, nb, blk, dh)

    qb, kb, vb = sub(q), sub(k), sub(v)

    def with_prev(t):
        prev = jnp.pad(t, ((0, 0), (0, 0), (0, 0), (1, 0), (0, 0), (0, 0)))[:, :, :, :-1]
        return jnp.concatenate([prev, t], axis=4)

    kc, vc = with_prev(kb), with_prev(vb)
    s = jnp.einsum('bhrnqd,bhrnkd->bhrnqk', qb, kc)
    qpos = jnp.arange(blk)[:, None] + blk
    kpos = jnp.arange(2 * blk)[None, :]
    dist = qpos - kpos
    band = (dist >= 0) & (dist <= blk)
    first_ok = (jnp.arange(nb)[:, None, None] > 0) | (kpos[None] >= blk)
    mask = band[None] & first_ok
    s = jnp.where(mask, s, -jnp.inf)
    m = jnp.max(s, axis=-1, keepdims=True)
    p = jnp.exp(s - m)
    denom = jnp.sum(p, axis=-1, keepdims=True)
    o = jnp.einsum('bhrnqk,bhrnkd->bhrnqd', p, vc) / denom

    def unsub(t):
        c = t.shape[-1]
        t = t.reshape(B, H, dilation, Lp, c)[:, :, :, :L]
        return t.transpose(0, 3, 2, 1, 4).reshape(B, S, H, c)

    return unsub(o), unsub(m), unsub(denom)


def dilated_attention(q, k, v):
    B, S, _ = q.shape
    f32 = jnp.float32
    q = q.astype(f32).reshape(B, S, ATTN_HEADS, ATTN_DH) * (ATTN_DH ** -0.5)
    k = k.astype(f32).reshape(B, S, ATTN_HEADS, ATTN_DH)
    v = v.astype(f32).reshape(B, S, ATTN_HEADS, ATTN_DH)
    branches = [dilated_branch(q, k, v, w, d) for (w, d) in DILATED_PATTERNS]
    m_all = jnp.max(jnp.stack([b[1] for b in branches]), axis=0)
    weights = [b[2] * jnp.exp(b[1] - m_all) for b in branches]
    num = sum(wt * b[0] for wt, b in zip(weights, branches))
    o = num / sum(weights)
    return o.reshape(B, S, ATTN_WIDTH)


def hybrid_mixer(x, w_in, w_out, lb, hgrn_norm_w):
    proj = jnp.einsum('bsd,dn->bsn', x, w_in)
    hq, hf, hi, hg, aq, ak, av = jnp.split(proj, IN_SPLITS, axis=-1)
    o_rec = hgrn2_mixer(hq, hf, hi, hg, lb, hgrn_norm_w)
    o_att = dilated_attention(aq, ak, av)
    o = jnp.concatenate([o_rec, o_att], axis=-1).astype(x.dtype)
    return jnp.einsum('bsn,nd->bsd', o, w_out)


def sq_relu_mlp(x, w1, w2):
    h = jnp.square(jax.nn.relu(jnp.einsum('bsd,df->bsf', x, w1)))
    return jnp.einsum('bsf,fd->bsd', h, w2)


def setup_inputs(seed: int = 0) -> dict:
    key = jax.random.key(seed)
    ks = jax.random.split(key, 12)
    f32 = jnp.float32
    x = jax.random.normal(ks[0], (BATCH, SEQ, D_MODEL), f32)
    w_in = jax.random.normal(ks[1], (DEPTH, D_MODEL, IN_COLS), f32) * D_MODEL ** -0.5
    w_out = jax.random.normal(ks[2], (DEPTH, D_MODEL, D_MODEL), f32) * (D_MODEL ** -0.5 * DEEPNORM_BETA)
    lower_bounds = 0.1 * jax.random.normal(ks[3], (DEPTH, HGRN_WIDTH), f32)
    hgrn_norm_w = 1.0 + 0.02 * jax.random.normal(ks[4], (DEPTH, HGRN_WIDTH), f32)
    ln1_g = 1.0 + 0.02 * jax.random.normal(ks[5], (DEPTH, D_MODEL), f32)
    ln1_b = 0.02 * jax.random.normal(ks[6], (DEPTH, D_MODEL), f32)
    w_ff1 = jax.random.normal(ks[7], (DEPTH, D_MODEL, D_FF), f32) * D_MODEL ** -0.5
    w_ff2 = jax.random.normal(ks[8], (DEPTH, D_FF, D_MODEL), f32) * (D_FF ** -0.5 * DEEPNORM_BETA)
    ln2_g = 1.0 + 0.02 * jax.random.normal(ks[9], (DEPTH, D_MODEL), f32)
    ln2_b = 0.02 * jax.random.normal(ks[10], (DEPTH, D_MODEL), f32)
    return {"x": x, "w_in": w_in, "w_out": w_out, "lower_bounds": lower_bounds,
            "hgrn_norm_w": hgrn_norm_w, "ln1_g": ln1_g, "ln1_b": ln1_b,
            "w_ff1": w_ff1, "w_ff2": w_ff2, "ln2_g": ln2_g, "ln2_b": ln2_b}


def reference(x, w_in, w_out, lower_bounds, hgrn_norm_w, ln1_g, ln1_b, w_ff1, w_ff2, ln2_g, ln2_b):
    lbs = jnp.cumsum(jax.nn.softmax(lower_bounds.astype(jnp.float32), axis=0), axis=0)
    lbs = lbs - lbs[0:1]
    for l in range(DEPTH):
        y = hybrid_mixer(x, w_in[l], w_out[l], lbs[l], hgrn_norm_w[l])
        x = layer_norm(DEEPNORM_ALPHA * x + y, ln1_g[l], ln1_b[l])
        y = sq_relu_mlp(x, w_ff1[l], w_ff2[l])
        x = layer_norm(DEEPNORM_ALPHA * x + y, ln2_g[l], ln2_b[l])
    return x
```

```python
import functools

import numpy as np
import jax
import jax.numpy as jnp
from jax import lax
from jax.experimental import pallas as pl
from jax.experimental.pallas import tpu as pltpu

F32 = jnp.float32
BF16 = jnp.bfloat16

D_MODEL = 1024
HGRN_WIDTH = 512
HGRN_HEADS = 4
HGRN_DK = 128
ATTN_WIDTH = 512
ATTN_HEADS = 8
ATTN_DH = 64
ATTN_BLK = 128
DILATIONS = (16, 4, 1)
D_FF = 4 * D_MODEL
IN_COLS = 4 * HGRN_WIDTH + 3 * ATTN_WIDTH
DEPTH = 2
DEEPNORM_ALPHA = (2 * DEPTH) ** 0.25
LN_EPS = 1e-5
RMS_EPS = 1e-6

HGRN_CHUNK = 64
HGRN_LEVELS = 6
HGRN_ROWS = 512
TOKEN_TILE = 512
VMEM_LIMIT = 48 * 1024 * 1024

NT_DIMS = (((1,), (1,)), ((), ()))
TN_DIMS = (((0,), (0,)), ((), ()))


def _dot(a, b):
    return jnp.dot(a, b, preferred_element_type=F32)


def _in_proj_kernel(x_ref, w_ref, hg_ref, q_ref, k_ref, v_ref):
    xb = x_ref[...].astype(BF16)
    for c in range(4):
        cols = slice(c * HGRN_WIDTH, (c + 1) * HGRN_WIDTH)
        hg_ref[:, cols] = _dot(xb, w_ref[:, cols])
    base = 4 * HGRN_WIDTH
    q = _dot(xb, w_ref[:, base:base + ATTN_WIDTH]) * (ATTN_DH ** -0.5)
    q_ref[...] = q.astype(BF16)
    k_ref[...] = _dot(xb, w_ref[:, base + ATTN_WIDTH:base + 2 * ATTN_WIDTH]).astype(BF16)
    v_ref[...] = _dot(xb, w_ref[:, base + 2 * ATTN_WIDTH:base + 3 * ATTN_WIDTH]).astype(BF16)


def _in_proj(x2d, w_in_bf16):
    t = x2d.shape[0]
    tm = TOKEN_TILE
    row = lambda i: (i, 0)
    return pl.pallas_call(
        _in_proj_kernel,
        grid=(t // tm,),
        in_specs=[pl.BlockSpec((tm, D_MODEL), row),
                  pl.BlockSpec((D_MODEL, IN_COLS), lambda i: (0, 0))],
        out_specs=[pl.BlockSpec((tm, 4 * HGRN_WIDTH), row),
                   pl.BlockSpec((tm, ATTN_WIDTH), row),
                   pl.BlockSpec((tm, ATTN_WIDTH), row),
                   pl.BlockSpec((tm, ATTN_WIDTH), row)],
        out_shape=[jax.ShapeDtypeStruct((t, 4 * HGRN_WIDTH), F32),
                   jax.ShapeDtypeStruct((t, ATTN_WIDTH), BF16),
                   jax.ShapeDtypeStruct((t, ATTN_WIDTH), BF16),
                   jax.ShapeDtypeStruct((t, ATTN_WIDTH), BF16)],
        compiler_params=pltpu.CompilerParams(dimension_semantics=("arbitrary",),
                                             vmem_limit_bytes=VMEM_LIMIT),
        name="in_proj",
    )(x2d, w_in_bf16)


def _hgrn_tables():
    c, nl = HGRN_CHUNK, HGRN_LEVELS
    t = np.arange(c)[:, None]
    u = np.arange(c)[None, :]
    tabs = []
    for l in range(nl):
        h = 1 << l
        upper = ((t >> l) & 1) == 1
        r = ((t >> (l + 1)) << (l + 1)) + h - 1
        tabs.append(np.where(upper, (u > r) & (u <= t), (u > t) & (u <= r)))
    tabs.append(u <= t)
    tabs.append(u > t)
    w = np.concatenate(tabs, axis=0).astype(np.float32)
    w3 = np.concatenate([w, w, w], axis=1)
    x = t ^ u
    lvl = np.where(u < t, np.floor(np.log2(np.maximum(x, 1))), np.where(u == t, nl, nl + 1))
    return jnp.asarray(w3, BF16), jnp.asarray(lvl, jnp.int32)


def _hgrn_kernel(layer, q_ref, f_ref, i_ref, g_ref, lbraw_ref, nw_ref, w3_ref, lvl_ref,
                 o_ref, state_ref):
    c, nl = HGRN_CHUNK, HGRN_LEVELS

    @pl.when(pl.program_id(2) == 0)
    def _():
        state_ref[...] = jnp.zeros_like(state_ref)

    raw = lbraw_ref[...]
    e = jnp.exp(raw - jnp.max(raw, axis=0, keepdims=True))
    sm = e / jnp.sum(e, axis=0, keepdims=True)
    first = sm[0:1, :]
    cum = first
    for j in range(1, layer + 1):
        cum = cum + sm[j:j + 1, :]
    lb = cum - first
    log_lb = jnp.log(lb)
    log_1mlb = jnp.log1p(-lb)
    one_m_lb = 1.0 - lb

    nw = nw_ref[...]
    w3 = w3_ref[...]
    lvl = lvl_ref[...]
    row = lax.broadcasted_iota(jnp.int32, (c, HGRN_DK), 0)
    upper = [((row >> l) & 1) == 1 for l in range(nl)]

    def chunk(ci, carry):
        rows = pl.ds(pl.multiple_of(ci * c, c), c)
        qr = q_ref[rows, :]
        z = f_ref[rows, :]
        v = i_ref[rows, :].astype(BF16)
        q = qr / (1.0 + jnp.exp(-qr))
        ez = jnp.exp(-jnp.abs(z))
        log_sig = jnp.minimum(z, 0.0) - jnp.log1p(ez)
        a = log_1mlb + log_sig
        lf = jnp.maximum(log_lb, a) + jnp.log1p(jnp.exp(-jnp.abs(log_lb - a)))
        rcp = 1.0 / (1.0 + ez)
        k = one_m_lb * jnp.where(z >= 0.0, ez * rcp, rcp)

        hi = lf.astype(BF16)
        r1 = lf - hi.astype(F32)
        mid = r1.astype(BF16)
        lo = (r1 - mid.astype(F32)).astype(BF16)
        ex = jnp.exp(_dot(w3, jnp.concatenate([hi, mid, lo], axis=0)))

        qb = q.astype(BF16)
        kb = k.astype(BF16)
        amat = jnp.zeros((c, c), F32)
        for l in range(nl):
            x = (jnp.where(upper[l], q, k) * ex[l * c:(l + 1) * c]).astype(BF16)
            s = lax.dot_general(x, x, NT_DIMS, preferred_element_type=F32)
            amat = jnp.where(lvl == l, s, amat)
        sd = lax.dot_general(qb, kb, NT_DIMS, preferred_element_type=F32)
        amat = jnp.where(lvl == nl, sd, amat)
        o = _dot(amat.astype(BF16), v)

        st = state_ref[...]
        e_in = ex[nl * c:(nl + 1) * c]
        qs = (q * e_in).astype(BF16)
        o = o + lax.dot_general(qs, st.astype(BF16), NT_DIMS, preferred_element_type=F32)
        kd = (k * ex[(nl + 1) * c:(nl + 2) * c]).astype(BF16)
        upd = lax.dot_general(v, kd, TN_DIMS, preferred_element_type=F32)
        state_ref[...] = st * e_in[c - 1:c, :] + upd

        o = o * lax.rsqrt(jnp.mean(o * o, axis=-1, keepdims=True) + RMS_EPS) * nw
        gate = g_ref[rows, :]
        o_ref[rows, :] = (o / (1.0 + jnp.exp(-gate))).astype(o_ref.dtype)
        return carry

    lax.fori_loop(0, q_ref.shape[0] // c, chunk, 0)


def _hgrn(hg3d, lower_bounds, norm_w_row, layer):
    b, s, _ = hg3d.shape
    rows = HGRN_ROWS
    w3, lvl = _hgrn_tables()
    nh = HGRN_HEADS

    def part(p):
        return pl.BlockSpec((None, rows, HGRN_DK), lambda bi, h, i: (bi, i, p * nh + h))

    full = lambda shape: pl.BlockSpec(shape, lambda bi, h, i: (0, 0))
    return pl.pallas_call(
        functools.partial(_hgrn_kernel, layer),
        grid=(b, nh, s // rows),
        in_specs=[part(0), part(1), part(2), part(3),
                  pl.BlockSpec((DEPTH, HGRN_DK), lambda bi, h, i: (0, h)),
                  pl.BlockSpec((1, HGRN_DK), lambda bi, h, i: (0, h)),
                  full(w3.shape), full(lvl.shape)],
        out_specs=pl.BlockSpec((None, rows, HGRN_DK), lambda bi, h, i: (bi, i, h)),
        out_shape=jax.ShapeDtypeStruct((b, s, HGRN_WIDTH), BF16),
        scratch_shapes=[pltpu.VMEM((HGRN_DK, HGRN_DK), F32)],
        compiler_params=pltpu.CompilerParams(
            dimension_semantics=("arbitrary", "arbitrary", "arbitrary"),
            vmem_limit_bytes=VMEM_LIMIT),
        name="hgrn2",
    )(hg3d, hg3d, hg3d, hg3d, lower_bounds, norm_w_row, w3, lvl)


def _attn_kernel(qb_count, rg, merge, *refs):
    if merge:
        (q_ref, k_ref, kp_ref, v_ref, vp_ref, o16_ref, l16_ref, o4_ref, l4_ref, o_ref) = refs
    else:
        (q_ref, k_ref, kp_ref, v_ref, vp_ref, o_ref, lse_ref) = refs
    blk = ATTN_BLK
    step = pl.program_id(1)
    qi = lax.broadcasted_iota(jnp.int32, (blk, 2 * blk), 0)
    kj = lax.broadcasted_iota(jnp.int32, (blk, 2 * blk), 1)
    band = (kj >= qi) & (kj <= qi + blk)
    neg = jnp.float32(-jnp.inf)
    bias_mid = jnp.where(band, 0.0, neg)
    bias_first = jnp.where(band & (kj >= blk), 0.0, neg)
    lane = lax.broadcasted_iota(jnp.int32, (blk, 2 * ATTN_DH), 1)
    low = lane < ATTN_DH
    ones = jnp.ones((2 * blk, 2 * ATTN_DH), BF16)
    zero_b = jnp.zeros((), BF16)

    for u in range(qb_count):
        r0 = u * blk
        if u == 0:
            bias = jnp.where(step == 0, bias_first, bias_mid)
        else:
            bias = bias_mid
        for r in range(rg):
            for hp in range(ATTN_HEADS // 2):
                cols = slice(r * ATTN_WIDTH + hp * 2 * ATTN_DH, r * ATTN_WIDTH + (hp + 1) * 2 * ATTN_DH)
                q2 = q_ref[r0:r0 + blk, cols]
                if u == 0:
                    kprev = kp_ref[:, cols]
                    vprev = vp_ref[:, cols]
                else:
                    kprev = k_ref[r0 - blk:r0, cols]
                    vprev = v_ref[r0 - blk:r0, cols]
                kcat = jnp.concatenate([kprev, k_ref[r0:r0 + blk, cols]], axis=0)
                vcat = jnp.concatenate([vprev, v_ref[r0:r0 + blk, cols]], axis=0)
                outs, lses = [], []
                for half in range(2):
                    qh = jnp.where(low if half == 0 else ~low, q2, zero_b)
                    s = lax.dot_general(qh, kcat, NT_DIMS, preferred_element_type=F32) + bias
                    m = jnp.max(s, axis=-1, keepdims=True)
                    p = jnp.exp(s - m).astype(BF16)
                    denom = _dot(p, ones)
                    outs.append(_dot(p, vcat) / denom)
                    lses.append(m + jnp.log(denom))
                o2 = jnp.where(low, outs[0], outs[1])
                lse2 = jnp.where(low, lses[0], lses[1])
                if merge:
                    la = l16_ref[r0:r0 + blk, cols]
                    lb = l4_ref[r0:r0 + blk, cols]
                    mx = jnp.maximum(jnp.maximum(la, lb), lse2)
                    wa = jnp.exp(la - mx)
                    wb = jnp.exp(lb - mx)
                    wc = jnp.exp(lse2 - mx)
                    num = (wa * o16_ref[r0:r0 + blk, cols].astype(F32)
                           + wb * o4_ref[r0:r0 + blk, cols].astype(F32) + wc * o2)
                    o_ref[r0:r0 + blk, cols] = (num / (wa + wb + wc)).astype(o_ref.dtype)
                else:
                    o_ref[r0:r0 + blk, cols] = o2
                    lse_ref[r0:r0 + blk, cols] = lse2


def _attn_pattern(q, k, v, dilation, merge_inputs=None):
    b, s, _ = q.shape
    d = dilation
    rows = s // d
    qb_count, rg = (4, 1) if d == 1 else (1, 4)
    width = rg * ATTN_WIDTH
    view = lambda a: a.reshape(b, rows, d * ATTN_WIDTH)
    cur = pl.BlockSpec((None, qb_count * ATTN_BLK, width), lambda bi, i, g: (bi, i, g))
    prev = pl.BlockSpec((None, ATTN_BLK, width),
                        lambda bi, i, g: (bi, jnp.maximum(i * qb_count - 1, 0), g))
    merge = merge_inputs is not None
    args = [view(q), view(k), view(k), view(v), view(v)]
    in_specs = [cur, cur, prev, cur, prev]
    if merge:
        args += [view(a) for a in merge_inputs]
        in_specs += [cur] * 4
        out_specs = cur
        out_shape = jax.ShapeDtypeStruct((b, rows, d * ATTN_WIDTH), BF16)
    else:
        out_specs = [cur, cur]
        out_shape = [jax.ShapeDtypeStruct((b, rows, d * ATTN_WIDTH), F32)] * 2
    out = pl.pallas_call(
        functools.partial(_attn_kernel, qb_count, rg, merge),
        grid=(b, rows // (qb_count * ATTN_BLK), d // rg),
        in_specs=in_specs,
        out_specs=out_specs,
        out_shape=out_shape,
        compiler_params=pltpu.CompilerParams(
            dimension_semantics=("arbitrary", "arbitrary", "arbitrary"),
            vmem_limit_bytes=VMEM_LIMIT),
        name=f"dilated_attn_d{d}",
    )(*args)
    if merge:
        return out.reshape(b, s, ATTN_WIDTH)
    return [a.reshape(b, s, ATTN_WIDTH) for a in out]


def _layer_norm(y, g, b):
    mu = jnp.mean(y, axis=-1, keepdims=True)
    yc = y - mu
    var = jnp.mean(yc * yc, axis=-1, keepdims=True)
    return yc * lax.rsqrt(var + LN_EPS) * g + b


def _out_mlp_kernel(x_ref, orec_ref, oatt_ref, wout_ref, g1_ref, b1_ref, w1_ref, w2_ref,
                    g2_ref, b2_ref, y_ref):
    x = x_ref[...]
    mix = (_dot(orec_ref[...], wout_ref[0:HGRN_WIDTH, :])
           + _dot(oatt_ref[...], wout_ref[HGRN_WIDTH:D_MODEL, :]))
    x1 = _layer_norm(DEEPNORM_ALPHA * x + mix, g1_ref[...], b1_ref[...])
    x1b = x1.astype(BF16)
    ff_chunk = 1024
    acc = jnp.zeros(x1.shape, F32)
    for c in range(D_FF // ff_chunk):
        cols = slice(c * ff_chunk, (c + 1) * ff_chunk)
        h = jnp.maximum(_dot(x1b, w1_ref[:, cols]), 0.0)
        acc = acc + _dot((h * h).astype(BF16), w2_ref[cols, :])
    y_ref[...] = _layer_norm(DEEPNORM_ALPHA * x1 + acc, g2_ref[...], b2_ref[...])


def _out_mlp(x2d, o_rec, o_att, w_out, g1, b1, w1, w2, g2, b2):
    t = x2d.shape[0]
    tm = TOKEN_TILE
    row = lambda i: (i, 0)
    const = lambda shape: pl.BlockSpec(shape, lambda i: (0, 0), pipeline_mode=pl.Buffered(1))
    return pl.pallas_call(
        _out_mlp_kernel,
        grid=(t // tm,),
        in_specs=[pl.BlockSpec((tm, D_MODEL), row),
                  pl.BlockSpec((tm, HGRN_WIDTH), row),
                  pl.BlockSpec((tm, ATTN_WIDTH), row),
                  const((D_MODEL, D_MODEL)), const((1, D_MODEL)), const((1, D_MODEL)),
                  const((D_MODEL, D_FF)), const((D_FF, D_MODEL)),
                  const((1, D_MODEL)), const((1, D_MODEL))],
        out_specs=pl.BlockSpec((tm, D_MODEL), row),
        out_shape=jax.ShapeDtypeStruct((t, D_MODEL), F32),
        compiler_params=pltpu.CompilerParams(dimension_semantics=("arbitrary",),
                                             vmem_limit_bytes=VMEM_LIMIT),
        name="out_proj_mlp",
    )(x2d, o_rec, o_att, w_out, g1, b1, w1, w2, g2, b2)


def kernel(x, w_in, w_out, lower_bounds, hgrn_norm_w, ln1_g, ln1_b, w_ff1, w_ff2, ln2_g, ln2_b):
    b, s, _ = x.shape
    t = b * s
    x2d = x.reshape(t, D_MODEL)
    row = lambda a, l: a[l].reshape(1, -1).astype(F32)
    for l in range(DEPTH):
        hg, q, k, v = _in_proj(x2d, w_in[l].astype(BF16))
        o_rec = _hgrn(hg.reshape(b, s, 4 * HGRN_WIDTH), lower_bounds.astype(F32),
                      row(hgrn_norm_w, l), l)
        q, k, v = (a.reshape(b, s, ATTN_WIDTH) for a in (q, k, v))
        o16, l16 = _attn_pattern(q, k, v, 16)
        o4, l4 = _attn_pattern(q, k, v, 4)
        o_att = _attn_pattern(q, k, v, 1, merge_inputs=(o16, l16, o4, l4))
        x2d = _out_mlp(x2d, o_rec.reshape(t, HGRN_WIDTH), o_att.reshape(t, ATTN_WIDTH),
                       w_out[l].astype(BF16), row(ln1_g, l), row(ln1_b, l),
                       w_ff1[l].astype(BF16), w_ff2[l].astype(BF16), row(ln2_g, l), row(ln2_b, l))
    return x2d.reshape(b, s, D_MODEL)
```

```python
import functools

import numpy as np
import jax
import jax.numpy as jnp
from jax import lax
from jax.experimental import pallas as pl
from jax.experimental.pallas import tpu as pltpu

F32 = jnp.float32
BF16 = jnp.bfloat16

D_MODEL = 1024
HGRN_WIDTH = 512
HGRN_HEADS = 4
HGRN_DK = 128
ATTN_WIDTH = 512
ATTN_HEADS = 8
ATTN_DH = 64
ATTN_PAIRS = ATTN_HEADS // 2
ATTN_BLK = 128
ATTN_SUPER = 2048
ATTN_GROUP = 256
ATTN_D1_ROWS = 1024
D_FF = 4 * D_MODEL
IN_COLS = 4 * HGRN_WIDTH + 3 * ATTN_WIDTH
DEPTH = 2
DEEPNORM_ALPHA = (2 * DEPTH) ** 0.25
LN_EPS = 1e-5
RMS_EPS = 1e-6

HGRN_CHUNK = 64
HGRN_LEVELS = 6
HGRN_ITER_CHUNKS = 2
HGRN_ROWS = 512
TOKEN_TILE = 512
VMEM_LIMIT = 48 * 1024 * 1024

NT_DIMS = (((1,), (1,)), ((), ()))
TN_DIMS = (((0,), (0,)), ((), ()))


def _dot(a, b):
    return jnp.dot(a, b, preferred_element_type=F32)


def _in_proj_kernel(x_ref, w_ref, hg_ref, q_ref, k_ref, v_ref):
    xb = x_ref[...].astype(BF16)
    for c in range(4):
        cols = slice(c * HGRN_WIDTH, (c + 1) * HGRN_WIDTH)
        hg_ref[:, cols] = _dot(xb, w_ref[:, cols])
    base = 4 * HGRN_WIDTH
    q = _dot(xb, w_ref[:, base:base + ATTN_WIDTH]) * (ATTN_DH ** -0.5)
    q_ref[...] = q.astype(BF16)
    k_ref[...] = _dot(xb, w_ref[:, base + ATTN_WIDTH:base + 2 * ATTN_WIDTH]).astype(BF16)
    v_ref[...] = _dot(xb, w_ref[:, base + 2 * ATTN_WIDTH:base + 3 * ATTN_WIDTH]).astype(BF16)


def _in_proj(x2d, w_in_bf16):
    t = x2d.shape[0]
    tm = TOKEN_TILE
    row = lambda i: (i, 0)
    return pl.pallas_call(
        _in_proj_kernel,
        grid=(t // tm,),
        in_specs=[pl.BlockSpec((tm, D_MODEL), row),
                  pl.BlockSpec((D_MODEL, IN_COLS), lambda i: (0, 0))],
        out_specs=[pl.BlockSpec((tm, 4 * HGRN_WIDTH), row),
                   pl.BlockSpec((tm, ATTN_WIDTH), row),
                   pl.BlockSpec((tm, ATTN_WIDTH), row),
                   pl.BlockSpec((tm, ATTN_WIDTH), row)],
        out_shape=[jax.ShapeDtypeStruct((t, 4 * HGRN_WIDTH), F32),
                   jax.ShapeDtypeStruct((t, ATTN_WIDTH), BF16),
                   jax.ShapeDtypeStruct((t, ATTN_WIDTH), BF16),
                   jax.ShapeDtypeStruct((t, ATTN_WIDTH), BF16)],
        compiler_params=pltpu.CompilerParams(dimension_semantics=("arbitrary",),
                                             vmem_limit_bytes=VMEM_LIMIT),
        name="in_proj",
    )(x2d, w_in_bf16)


def _hgrn_tables():
    c, nl = HGRN_CHUNK, HGRN_LEVELS
    t = np.arange(c)[:, None]
    u = np.arange(c)[None, :]
    tabs = []
    for l in range(nl):
        h = 1 << l
        upper = ((t >> l) & 1) == 1
        r = ((t >> (l + 1)) << (l + 1)) + h - 1
        tabs.append(np.where(upper, (u > r) & (u <= t), (u > t) & (u <= r)))
    tabs.append(u <= t)
    tabs.append(u > t)
    w = np.concatenate(tabs, axis=0).astype(np.float32)
    w3 = np.concatenate([w, w, w], axis=1)
    x = t ^ u
    lvl = np.where(u < t, np.floor(np.log2(np.maximum(x, 1))), np.where(u == t, nl, nl + 1))
    return jnp.asarray(w3, BF16), jnp.asarray(lvl, jnp.int32)


def _hgrn_kernel(layer, q_ref, f_ref, i_ref, g_ref, lbraw_ref, nw_ref, w3_ref, lvl_ref,
                 o_ref, state_ref):
    c, nl, nh, dk = HGRN_CHUNK, HGRN_LEVELS, HGRN_HEADS, HGRN_DK
    head = lambda a, h: a[:, h * dk:(h + 1) * dk]

    @pl.when(pl.program_id(1) == 0)
    def _():
        state_ref[...] = jnp.zeros_like(state_ref)

    raw = lbraw_ref[...]
    e = jnp.exp(raw - jnp.max(raw, axis=0, keepdims=True))
    sm = e / jnp.sum(e, axis=0, keepdims=True)
    first = sm[0:1, :]
    cum = first
    for j in range(1, layer + 1):
        cum = cum + sm[j:j + 1, :]
    lb = cum - first
    log_lb = jnp.log(lb)
    log_1mlb = jnp.log1p(-lb)
    one_m_lb = 1.0 - lb

    nw = nw_ref[...]
    w3 = w3_ref[...]
    lvl = lvl_ref[...]
    row = lax.broadcasted_iota(jnp.int32, (c, HGRN_WIDTH), 0)
    upper = [((row >> l) & 1) == 1 for l in range(nl)]

    def iteration(it, carry):
        chunks = []
        for ch in range(HGRN_ITER_CHUNKS):
            rows = pl.ds(pl.multiple_of((it * HGRN_ITER_CHUNKS + ch) * c, c), c)
            qr = q_ref[rows, :]
            z = f_ref[rows, :]
            v = i_ref[rows, :].astype(BF16)
            q = qr / (1.0 + jnp.exp(-qr))
            ez = jnp.exp(-jnp.abs(z))
            log_sig = jnp.minimum(z, 0.0) - jnp.log1p(ez)
            a = log_1mlb + log_sig
            lf = jnp.maximum(log_lb, a) + jnp.log1p(jnp.exp(-jnp.abs(log_lb - a)))
            rcp = 1.0 / (1.0 + ez)
            k = one_m_lb * jnp.where(z >= 0.0, ez * rcp, rcp)
            hi = lf.astype(BF16)
            r1 = lf - hi.astype(F32)
            mid = r1.astype(BF16)
            lo = (r1 - mid.astype(F32)).astype(BF16)
            ex = jnp.exp(_dot(w3, jnp.concatenate([hi, mid, lo], axis=0)))
            chunks.append((rows, q, k, v, ex))

        amats = []
        for rows, q, k, v, ex in chunks:
            xs = [(jnp.where(upper[l], q, k) * ex[l * c:(l + 1) * c]).astype(BF16) for l in range(nl)]
            qb = q.astype(BF16)
            kb = k.astype(BF16)
            per_head = []
            for h in range(nh):
                amat = lax.dot_general(head(qb, h), head(kb, h), NT_DIMS, preferred_element_type=F32)
                amat = jnp.where(lvl == nl, amat, 0.0)
                for l in range(nl):
                    x = head(xs[l], h)
                    s = lax.dot_general(x, x, NT_DIMS, preferred_element_type=F32)
                    amat = jnp.where(lvl == l, s, amat)
                per_head.append(amat.astype(BF16))
            amats.append(per_head)

        upds = []
        for rows, q, k, v, ex in chunks:
            kd = (k * ex[(nl + 1) * c:(nl + 2) * c]).astype(BF16)
            upds.append([lax.dot_general(head(v, h), head(kd, h), TN_DIMS, preferred_element_type=F32)
                         for h in range(nh)])

        before = []
        for h in range(nh):
            st = state_ref[h]
            seen = []
            for ci, (rows, q, k, v, ex) in enumerate(chunks):
                seen.append(st.astype(BF16))
                st = st * head(ex[(nl + 1) * c - 1:(nl + 1) * c], h) + upds[ci][h]
            state_ref[h] = st
            before.append(seen)

        for ci, (rows, q, k, v, ex) in enumerate(chunks):
            qs = (q * ex[nl * c:(nl + 1) * c]).astype(BF16)
            outs = []
            for h in range(nh):
                o = _dot(amats[ci][h], head(v, h))
                o = o + lax.dot_general(head(qs, h), before[h][ci], NT_DIMS, preferred_element_type=F32)
                outs.append(o * lax.rsqrt(jnp.mean(o * o, axis=-1, keepdims=True) + RMS_EPS))
            o = jnp.concatenate(outs, axis=1) * nw
            gate = g_ref[rows, :]
            o_ref[rows, :] = (o / (1.0 + jnp.exp(-gate))).astype(o_ref.dtype)
        return carry

    lax.fori_loop(0, q_ref.shape[0] // (c * HGRN_ITER_CHUNKS), iteration, 0)


def _hgrn(hg3d, lower_bounds, norm_w_row, layer):
    b, s, _ = hg3d.shape
    rows = HGRN_ROWS
    w3, lvl = _hgrn_tables()
    part = lambda p: pl.BlockSpec((None, rows, HGRN_WIDTH), lambda bi, i: (bi, i, p))
    full = lambda shape: pl.BlockSpec(shape, lambda bi, i: (0, 0))
    return pl.pallas_call(
        functools.partial(_hgrn_kernel, layer),
        grid=(b, s // rows),
        in_specs=[part(0), part(1), part(2), part(3),
                  full((DEPTH, HGRN_WIDTH)), full((1, HGRN_WIDTH)), full(w3.shape), full(lvl.shape)],
        out_specs=pl.BlockSpec((None, rows, HGRN_WIDTH), lambda bi, i: (bi, i, 0)),
        out_shape=jax.ShapeDtypeStruct((b, s, HGRN_WIDTH), BF16),
        scratch_shapes=[pltpu.VMEM((HGRN_HEADS, HGRN_DK, HGRN_DK), F32)],
        compiler_params=pltpu.CompilerParams(dimension_semantics=("arbitrary", "arbitrary"),
                                             vmem_limit_bytes=VMEM_LIMIT),
        name="hgrn2",
    )(hg3d, hg3d, hg3d, hg3d, lower_bounds, norm_w_row, w3, lvl)


def _attn_tables(d):
    g = ATTN_GROUP // d
    p = np.zeros((ATTN_GROUP, ATTN_GROUP), np.float32)
    for r in range(d):
        for m in range(g):
            p[r * g + m, d * m + r] = 1.0
    return jnp.asarray(p, BF16), jnp.asarray(p.T, BF16)


def _lse_expand_table():
    e = np.zeros((ATTN_PAIRS, 2 * ATTN_BLK, 2 * ATTN_DH), np.float32)
    for hp in range(ATTN_PAIRS):
        for dst in range(2 * ATTN_DH):
            src = (dst // ATTN_DH) * ATTN_DH + 16 * hp
            e[hp, src, dst] = 1.0
            e[hp, ATTN_BLK + src, dst] = 1.0
    return jnp.asarray(e, BF16)


def _attn_consts():
    blk = ATTN_BLK
    qi = lax.broadcasted_iota(jnp.int32, (blk, 2 * blk), 0)
    kj = lax.broadcasted_iota(jnp.int32, (blk, 2 * blk), 1)
    band = (kj >= qi) & (kj <= qi + blk)
    neg = jnp.float32(-jnp.inf)
    bias_mid = jnp.where(band, 0.0, neg)
    bias_first = jnp.where(band & (kj >= blk), 0.0, neg)
    lane = lax.broadcasted_iota(jnp.int32, (blk, 2 * ATTN_DH), 1)
    return bias_mid, bias_first, lane


def _attn_scores(load_q, kcats, bias, low):
    zero = jnp.zeros((), BF16)
    out = []
    for hp in range(ATTN_PAIRS):
        q2 = load_q(hp)
        for half in range(2):
            qh = jnp.where(low if half == 0 else ~low, q2, zero)
            out.append(lax.dot_general(qh, kcats[hp], NT_DIMS, preferred_element_type=F32) + bias)
    return out


def _attn_softmax_pv(scores, vcats, low):
    one = jnp.ones((), BF16)
    lowv = lax.broadcasted_iota(jnp.int32, (2 * ATTN_BLK, 2 * ATTN_DH), 1) < ATTN_DH
    maxes = [jnp.max(s, axis=-1, keepdims=True) for s in scores]
    probs = [jnp.exp(s - m).astype(BF16) for s, m in zip(scores, maxes)]
    res = []
    for hp in range(ATTN_PAIRS):
        ra = _dot(probs[2 * hp], jnp.where(lowv, vcats[hp], one))
        rb = _dot(probs[2 * hp + 1], jnp.where(lowv, one, vcats[hp]))
        den = pltpu.roll(jnp.where(low, rb, ra), ATTN_DH, axis=1)
        out = jnp.where(low, ra, rb) / den
        lse = jnp.where(low, maxes[2 * hp], maxes[2 * hp + 1]) + jnp.log(den)
        res.append((out, lse))
    return res


def _attn_strided_kernel(d, q_ref, k_ref, v_ref, perm_ref, permt_ref, o_ref, lsec_ref,
                         qs, ks, vs, kp, vp, ost, lst):
    blk = ATTN_BLK
    grp = ATTN_GROUP // d
    rrows = ATTN_SUPER // d
    nj = rrows // blk
    ngroups = ATTN_SUPER // ATTN_GROUP
    step = pl.program_id(1)

    @pl.when(step == 0)
    def _():
        kp[...] = jnp.zeros_like(kp)
        vp[...] = jnp.zeros_like(vp)

    perm = perm_ref[...]
    for src, dst in ((q_ref, qs), (k_ref, ks), (v_ref, vs)):
        for g in range(ngroups):
            y = _dot(perm, src[g * ATTN_GROUP:(g + 1) * ATTN_GROUP, :]).astype(BF16)
            for r in range(d):
                dst[r, g * grp:(g + 1) * grp, :] = y[r * grp:(r + 1) * grp, :]

    bias_mid, bias_first, lane = _attn_consts()
    low = lane < ATTN_DH
    lane_grp = (lane % ATTN_DH) // 16
    units = [(r, j) for r in range(d) for j in range(nj)]

    def cat(cur, prev, r, j, hp):
        cols = slice(hp * 2 * ATTN_DH, (hp + 1) * 2 * ATTN_DH)
        before = prev[r, :, cols] if j == 0 else cur[r, (j - 1) * blk:j * blk, cols]
        return jnp.concatenate([before, cur[r, j * blk:(j + 1) * blk, cols]], axis=0)

    def scores(u):
        r, j = units[u]
        bias = jnp.where(step == 0, bias_first, bias_mid) if j == 0 else bias_mid
        kcats = [cat(ks, kp, r, j, hp) for hp in range(ATTN_PAIRS)]
        load_q = lambda hp: qs[r, j * blk:(j + 1) * blk, hp * 2 * ATTN_DH:(hp + 1) * 2 * ATTN_DH]
        return _attn_scores(load_q, kcats, bias, low)

    def finish(u, sc):
        r, j = units[u]
        rows = slice(j * blk, (j + 1) * blk)
        res = _attn_softmax_pv(sc, [cat(vs, vp, r, j, hp) for hp in range(ATTN_PAIRS)], low)
        comp = res[ATTN_PAIRS - 1][1]
        for hp in range(ATTN_PAIRS - 2, -1, -1):
            comp = jnp.where(lane_grp == hp, res[hp][1], comp)
        for hp in range(ATTN_PAIRS):
            ost[r, rows, hp * 2 * ATTN_DH:(hp + 1) * 2 * ATTN_DH] = res[hp][0].astype(BF16)
        hi = comp.astype(BF16)
        lst[r, rows, 0:blk] = hi
        lst[r, rows, blk:2 * blk] = (comp - hi.astype(F32)).astype(BF16)

    nxt = scores(0)
    for u in range(len(units)):
        cur = nxt
        if u + 1 < len(units):
            nxt = scores(u + 1)
        finish(u, cur)

    permt = permt_ref[...]
    for g in range(ngroups):
        rows = slice(g * ATTN_GROUP, (g + 1) * ATTN_GROUP)
        og = jnp.concatenate([ost[r, g * grp:(g + 1) * grp, :] for r in range(d)], axis=0)
        o_ref[rows, :] = _dot(permt, og).astype(BF16)
        lg = jnp.concatenate([lst[r, g * grp:(g + 1) * grp, :] for r in range(d)], axis=0)
        lsec_ref[rows, :] = _dot(permt, lg).astype(BF16)

    kp[...] = ks[:, rrows - blk:rrows, :]
    vp[...] = vs[:, rrows - blk:rrows, :]


def _attn_merge_kernel(q_ref, k_ref, kp_ref, v_ref, vp_ref, o16_ref, l16_ref, o4_ref, l4_ref,
                       ee_ref, o_ref):
    blk = ATTN_BLK
    nq = ATTN_D1_ROWS // blk
    step = pl.program_id(1)
    bias_mid, bias_first, lane = _attn_consts()
    low = lane < ATTN_DH

    def cat(cur, prev, j, hp):
        cols = slice(hp * 2 * ATTN_DH, (hp + 1) * 2 * ATTN_DH)
        before = prev[:, cols] if j == 0 else cur[(j - 1) * blk:j * blk, cols]
        return jnp.concatenate([before, cur[j * blk:(j + 1) * blk, cols]], axis=0)

    def scores(j):
        bias = jnp.where(step == 0, bias_first, bias_mid) if j == 0 else bias_mid
        kcats = [cat(k_ref, kp_ref, j, hp) for hp in range(ATTN_PAIRS)]
        load_q = lambda hp: q_ref[j * blk:(j + 1) * blk, hp * 2 * ATTN_DH:(hp + 1) * 2 * ATTN_DH]
        return _attn_scores(load_q, kcats, bias, low)

    def finish(j, sc):
        rows = slice(j * blk, (j + 1) * blk)
        res = _attn_softmax_pv(sc, [cat(v_ref, vp_ref, j, hp) for hp in range(ATTN_PAIRS)], low)
        l16c = l16_ref[rows, :]
        l4c = l4_ref[rows, :]
        for hp in range(ATTN_PAIRS):
            cols = slice(hp * 2 * ATTN_DH, (hp + 1) * 2 * ATTN_DH)
            o1, lse1 = res[hp]
            lse16 = _dot(l16c, ee_ref[hp])
            lse4 = _dot(l4c, ee_ref[hp])
            mx = jnp.maximum(jnp.maximum(lse16, lse4), lse1)
            w16 = jnp.exp(lse16 - mx)
            w4 = jnp.exp(lse4 - mx)
            w1 = jnp.exp(lse1 - mx)
            num = (w16 * o16_ref[rows, cols].astype(F32) + w4 * o4_ref[rows, cols].astype(F32)
                   + w1 * o1)
            o_ref[rows, cols] = (num / (w16 + w4 + w1)).astype(o_ref.dtype)

    nxt = scores(0)
    for j in range(nq):
        cur = nxt
        if j + 1 < nq:
            nxt = scores(j + 1)
        finish(j, cur)


def _dilated_attention(q, k, v):
    b, s, _ = q.shape
    blk = ATTN_BLK
    sds = jax.ShapeDtypeStruct
    params = pltpu.CompilerParams(dimension_semantics=("arbitrary", "arbitrary"),
                                  vmem_limit_bytes=VMEM_LIMIT)
    partial_outs = []
    for d in (16, 4):
        perm, permt = _attn_tables(d)
        rrows = ATTN_SUPER // d
        tokens = lambda w: pl.BlockSpec((None, ATTN_SUPER, w), lambda bi, i: (bi, i, 0))
        table = pl.BlockSpec((ATTN_GROUP, ATTN_GROUP), lambda bi, i: (0, 0))
        residue_major = lambda rows, w: pltpu.VMEM((d, rows, w), BF16)
        partial_outs += pl.pallas_call(
            functools.partial(_attn_strided_kernel, d),
            grid=(b, s // ATTN_SUPER),
            in_specs=[tokens(ATTN_WIDTH)] * 3 + [table, table],
            out_specs=[tokens(ATTN_WIDTH), tokens(2 * blk)],
            out_shape=[sds((b, s, ATTN_WIDTH), BF16), sds((b, s, 2 * blk), BF16)],
            scratch_shapes=[residue_major(rrows, ATTN_WIDTH)] * 3
                           + [residue_major(blk, ATTN_WIDTH)] * 2
                           + [residue_major(rrows, ATTN_WIDTH), residue_major(rrows, 2 * blk)],
            compiler_params=params,
            name=f"dilated_attn_d{d}",
        )(q, k, v, perm, permt)

    nq = ATTN_D1_ROWS // blk
    tokens = lambda w: pl.BlockSpec((None, ATTN_D1_ROWS, w), lambda bi, i: (bi, i, 0))
    prev = pl.BlockSpec((None, blk, ATTN_WIDTH), lambda bi, i: (bi, jnp.maximum(i * nq - 1, 0), 0))
    ee = _lse_expand_table()
    return pl.pallas_call(
        _attn_merge_kernel,
        grid=(b, s // ATTN_D1_ROWS),
        in_specs=[tokens(ATTN_WIDTH), tokens(ATTN_WIDTH), prev, tokens(ATTN_WIDTH), prev,
                  tokens(ATTN_WIDTH), tokens(2 * blk), tokens(ATTN_WIDTH), tokens(2 * blk),
                  pl.BlockSpec(ee.shape, lambda bi, i: (0, 0, 0))],
        out_specs=tokens(ATTN_WIDTH),
        out_shape=sds((b, s, ATTN_WIDTH), BF16),
        compiler_params=params,
        name="dilated_attn_d1_merge",
    )(q, k, k, v, v, *partial_outs, ee)


def _layer_norm(y, g, b):
    mu = jnp.mean(y, axis=-1, keepdims=True)
    yc = y - mu
    var = jnp.mean(yc * yc, axis=-1, keepdims=True)
    return yc * lax.rsqrt(var + LN_EPS) * g + b


def _out_mlp_kernel(x_ref, orec_ref, oatt_ref, wout_ref, g1_ref, b1_ref, w1_ref, w2_ref,
                    g2_ref, b2_ref, y_ref):
    x = x_ref[...]
    mix = (_dot(orec_ref[...], wout_ref[0:HGRN_WIDTH, :])
           + _dot(oatt_ref[...], wout_ref[HGRN_WIDTH:D_MODEL, :]))
    x1 = _layer_norm(DEEPNORM_ALPHA * x + mix, g1_ref[...], b1_ref[...])
    x1b = x1.astype(BF16)
    ff_chunk = 1024
    acc = jnp.zeros(x1.shape, F32)
    for c in range(D_FF // ff_chunk):
        cols = slice(c * ff_chunk, (c + 1) * ff_chunk)
        h = jnp.maximum(_dot(x1b, w1_ref[:, cols]), 0.0)
        acc = acc + _dot((h * h).astype(BF16), w2_ref[cols, :])
    y_ref[...] = _layer_norm(DEEPNORM_ALPHA * x1 + acc, g2_ref[...], b2_ref[...])


def _out_mlp(x2d, o_rec, o_att, w_out, g1, b1, w1, w2, g2, b2):
    t = x2d.shape[0]
    tm = TOKEN_TILE
    row = lambda i: (i, 0)
    const = lambda shape: pl.BlockSpec(shape, lambda i: (0, 0), pipeline_mode=pl.Buffered(1))
    return pl.pallas_call(
        _out_mlp_kernel,
        grid=(t // tm,),
        in_specs=[pl.BlockSpec((tm, D_MODEL), row),
                  pl.BlockSpec((tm, HGRN_WIDTH), row),
                  pl.BlockSpec((tm, ATTN_WIDTH), row),
                  const((D_MODEL, D_MODEL)), const((1, D_MODEL)), const((1, D_MODEL)),
                  const((D_MODEL, D_FF)), const((D_FF, D_MODEL)),
                  const((1, D_MODEL)), const((1, D_MODEL))],
        out_specs=pl.BlockSpec((tm, D_MODEL), row),
        out_shape=jax.ShapeDtypeStruct((t, D_MODEL), F32),
        compiler_params=pltpu.CompilerParams(dimension_semantics=("arbitrary",),
                                             vmem_limit_bytes=VMEM_LIMIT),
        name="out_proj_mlp",
    )(x2d, o_rec, o_att, w_out, g1, b1, w1, w2, g2, b2)


def kernel(x, w_in, w_out, lower_bounds, hgrn_norm_w, ln1_g, ln1_b, w_ff1, w_ff2, ln2_g, ln2_b):
    b, s, _ = x.shape
    t = b * s
    x2d = x.reshape(t, D_MODEL)
    row = lambda a, l: a[l].reshape(1, -1).astype(F32)
    for l in range(DEPTH):
        hg, q, k, v = _in_proj(x2d, w_in[l].astype(BF16))
        o_rec = _hgrn(hg.reshape(b, s, 4 * HGRN_WIDTH), lower_bounds.astype(F32),
                      row(hgrn_norm_w, l), l)
        o_att = _dilated_attention(*(a.reshape(b, s, ATTN_WIDTH) for a in (q, k, v)))
        x2d = _out_mlp(x2d, o_rec.reshape(t, HGRN_WIDTH), o_att.reshape(t, ATTN_WIDTH),
                       w_out[l].astype(BF16), row(ln1_g, l), row(ln1_b, l),
                       w_ff1[l].astype(BF16), w_ff2[l].astype(BF16), row(ln2_g, l), row(ln2_b, l))
    return x2d.reshape(b, s, D_MODEL)
```

```python
import functools

import numpy as np
import jax
import jax.numpy as jnp
from jax import lax
from jax.experimental import pallas as pl
from jax.experimental.pallas import tpu as pltpu

F32 = jnp.float32
BF16 = jnp.bfloat16

D_MODEL = 1024
HGRN_WIDTH = 512
HGRN_HEADS = 4
HGRN_DK = 128
ATTN_WIDTH = 512
ATTN_HEADS = 8
ATTN_DH = 64
ATTN_PAIRS = ATTN_HEADS // 2
ATTN_BLK = 128
ATTN_SUPER = 2048
ATTN_GROUP = 256
ATTN_D1_ROWS = 1024
D_FF = 4 * D_MODEL
IN_COLS = 4 * HGRN_WIDTH + 3 * ATTN_WIDTH
DEPTH = 2
DEEPNORM_ALPHA = (2 * DEPTH) ** 0.25
LN_EPS = 1e-5
RMS_EPS = 1e-6
LOG2_E = 1.4426950408889634

HGRN_CHUNK = 64
HGRN_LEVELS = 6
HGRN_ITER_CHUNKS = 4
HGRN_ROWS = 512
TOKEN_TILE = 512
MLP_SUBTILE = 256
MLP_FF_CHUNK = 1024
VMEM_LIMIT = 48 * 1024 * 1024

NT_DIMS = (((1,), (1,)), ((), ()))
TN_DIMS = (((0,), (0,)), ((), ()))


def _dot(a, b):
    return jnp.dot(a, b, preferred_element_type=F32)


def _sigmoid(x):
    return 0.5 * jnp.tanh(0.5 * x) + 0.5


def _in_proj_kernel(x_ref, w_ref, hg_ref, q_ref, k_ref, v_ref):
    xb = x_ref[...].astype(BF16)
    for c in range(4):
        cols = slice(c * HGRN_WIDTH, (c + 1) * HGRN_WIDTH)
        hg_ref[:, cols] = _dot(xb, w_ref[:, cols])
    base = 4 * HGRN_WIDTH
    q = _dot(xb, w_ref[:, base:base + ATTN_WIDTH]) * (ATTN_DH ** -0.5)
    q_ref[...] = q.astype(BF16)
    k_ref[...] = _dot(xb, w_ref[:, base + ATTN_WIDTH:base + 2 * ATTN_WIDTH]).astype(BF16)
    v_ref[...] = _dot(xb, w_ref[:, base + 2 * ATTN_WIDTH:base + 3 * ATTN_WIDTH]).astype(BF16)


def _in_proj(x2d, w_in_bf16):
    t = x2d.shape[0]
    tm = TOKEN_TILE
    row = lambda i: (i, 0)
    return pl.pallas_call(
        _in_proj_kernel,
        grid=(t // tm,),
        in_specs=[pl.BlockSpec((tm, D_MODEL), row),
                  pl.BlockSpec((D_MODEL, IN_COLS), lambda i: (0, 0))],
        out_specs=[pl.BlockSpec((tm, 4 * HGRN_WIDTH), row),
                   pl.BlockSpec((tm, ATTN_WIDTH), row),
                   pl.BlockSpec((tm, ATTN_WIDTH), row),
                   pl.BlockSpec((tm, ATTN_WIDTH), row)],
        out_shape=[jax.ShapeDtypeStruct((t, 4 * HGRN_WIDTH), F32),
                   jax.ShapeDtypeStruct((t, ATTN_WIDTH), BF16),
                   jax.ShapeDtypeStruct((t, ATTN_WIDTH), BF16),
                   jax.ShapeDtypeStruct((t, ATTN_WIDTH), BF16)],
        compiler_params=pltpu.CompilerParams(dimension_semantics=("arbitrary",),
                                             vmem_limit_bytes=VMEM_LIMIT),
        name="in_proj",
    )(x2d, w_in_bf16)


def _hgrn_tables():
    c, nl = HGRN_CHUNK, HGRN_LEVELS
    t = np.arange(c)[:, None]
    u = np.arange(c)[None, :]
    tabs = []
    for l in range(nl):
        h = 1 << l
        upper = ((t >> l) & 1) == 1
        r = ((t >> (l + 1)) << (l + 1)) + h - 1
        tabs.append(np.where(upper, (u > r) & (u <= t), (u > t) & (u <= r)))
    tabs.append(u <= t)
    tabs.append(u > t)
    w = np.concatenate(tabs, axis=0).astype(np.float32)
    w3 = np.concatenate([w, w, w], axis=1)
    x = t ^ u
    lvl = np.where(u < t, np.floor(np.log2(np.maximum(x, 1))), np.where(u == t, nl, nl + 1))
    return jnp.asarray(w3, BF16), jnp.asarray(lvl, jnp.int32)


def _hgrn_kernel(layer, q_ref, f_ref, i_ref, g_ref, lbraw_ref, nw_ref, w3_ref, lvl_ref,
                 o_ref, state_ref):
    c, nl, nh, dk = HGRN_CHUNK, HGRN_LEVELS, HGRN_HEADS, HGRN_DK
    head = lambda a, h: a[:, h * dk:(h + 1) * dk]

    @pl.when(pl.program_id(1) == 0)
    def _():
        state_ref[...] = jnp.zeros_like(state_ref)

    raw = lbraw_ref[...]
    e = jnp.exp(raw - jnp.max(raw, axis=0, keepdims=True))
    sm = e / jnp.sum(e, axis=0, keepdims=True)
    first = sm[0:1, :]
    cum = first
    for j in range(1, layer + 1):
        cum = cum + sm[j:j + 1, :]
    lb = cum - first
    log_lb = jnp.log(lb)
    log_1mlb = jnp.log1p(-lb)
    one_m_lb = 1.0 - lb

    nw = nw_ref[...]
    w3 = w3_ref[...]
    lvl = lvl_ref[...]
    row = lax.broadcasted_iota(jnp.int32, (c, HGRN_WIDTH), 0)
    upper = [((row >> l) & 1) == 1 for l in range(nl)]

    def group(it):
        chunks = []
        for ch in range(HGRN_ITER_CHUNKS):
            start = (it * HGRN_ITER_CHUNKS + ch) * c
            rows = slice(start, start + c)
            qr = q_ref[rows, :]
            z = f_ref[rows, :]
            v = i_ref[rows, :].astype(BF16)
            q = qr * _sigmoid(qr)
            log_sig = jnp.minimum(z, 0.0) - jnp.log(1.0 + jnp.exp(-jnp.abs(z)))
            a = log_1mlb + log_sig
            lf = jnp.maximum(log_lb, a) + jnp.log(1.0 + jnp.exp(-jnp.abs(log_lb - a)))
            k = one_m_lb * _sigmoid(-z)
            lf2 = lf * LOG2_E
            hi = lf2.astype(BF16)
            r1 = lf2 - hi.astype(F32)
            mid = r1.astype(BF16)
            lo = (r1 - mid.astype(F32)).astype(BF16)
            ex = jnp.exp2(_dot(w3, jnp.concatenate([hi, mid, lo], axis=0)))
            chunks.append((rows, q, k, v, ex))

        amats = []
        for rows, q, k, v, ex in chunks:
            xs = [(jnp.where(upper[l], q, k) * ex[l * c:(l + 1) * c]).astype(BF16) for l in range(nl)]
            qb = q.astype(BF16)
            kb = k.astype(BF16)
            per_head = []
            for h in range(nh):
                amat = lax.dot_general(head(qb, h), head(kb, h), NT_DIMS, preferred_element_type=F32)
                amat = jnp.where(lvl == nl, amat, 0.0)
                for l in range(nl):
                    x = head(xs[l], h)
                    s = lax.dot_general(x, x, NT_DIMS, preferred_element_type=F32)
                    amat = jnp.where(lvl == l, s, amat)
                per_head.append(amat.astype(BF16))
            amats.append(per_head)

        upds = []
        for rows, q, k, v, ex in chunks:
            kd = (k * ex[(nl + 1) * c:(nl + 2) * c]).astype(BF16)
            upds.append([lax.dot_general(head(v, h), head(kd, h), TN_DIMS, preferred_element_type=F32)
                         for h in range(nh)])

        before = []
        for h in range(nh):
            st = state_ref[h]
            seen = []
            for ci, (rows, q, k, v, ex) in enumerate(chunks):
                seen.append(st.astype(BF16))
                st = st * head(ex[(nl + 1) * c - 1:(nl + 1) * c], h) + upds[ci][h]
            state_ref[h] = st
            before.append(seen)

        for ci, (rows, q, k, v, ex) in enumerate(chunks):
            qs = (q * ex[nl * c:(nl + 1) * c]).astype(BF16)
            outs = []
            for h in range(nh):
                o = _dot(amats[ci][h], head(v, h))
                o = o + lax.dot_general(head(qs, h), before[h][ci], NT_DIMS, preferred_element_type=F32)
                outs.append(o * lax.rsqrt(jnp.mean(o * o, axis=-1, keepdims=True) + RMS_EPS))
            o = jnp.concatenate(outs, axis=1) * nw
            gate = g_ref[rows, :]
            o_ref[rows, :] = (o * _sigmoid(gate)).astype(o_ref.dtype)

    for it in range(q_ref.shape[0] // (c * HGRN_ITER_CHUNKS)):
        group(it)


def _hgrn(hg3d, lower_bounds, norm_w_row, layer):
    b, s, _ = hg3d.shape
    rows = HGRN_ROWS
    w3, lvl = _hgrn_tables()
    part = lambda p: pl.BlockSpec((None, rows, HGRN_WIDTH), lambda bi, i: (bi, i, p))
    full = lambda shape: pl.BlockSpec(shape, lambda bi, i: (0, 0))
    return pl.pallas_call(
        functools.partial(_hgrn_kernel, layer),
        grid=(b, s // rows),
        in_specs=[part(0), part(1), part(2), part(3),
                  full((DEPTH, HGRN_WIDTH)), full((1, HGRN_WIDTH)), full(w3.shape), full(lvl.shape)],
        out_specs=pl.BlockSpec((None, rows, HGRN_WIDTH), lambda bi, i: (bi, i, 0)),
        out_shape=jax.ShapeDtypeStruct((b, s, HGRN_WIDTH), BF16),
        scratch_shapes=[pltpu.VMEM((HGRN_HEADS, HGRN_DK, HGRN_DK), F32)],
        compiler_params=pltpu.CompilerParams(dimension_semantics=("arbitrary", "arbitrary"),
                                             vmem_limit_bytes=VMEM_LIMIT),
        name="hgrn2",
    )(hg3d, hg3d, hg3d, hg3d, lower_bounds, norm_w_row, w3, lvl)


def _attn_tables(d):
    g = ATTN_GROUP // d
    p = np.zeros((ATTN_GROUP, ATTN_GROUP), np.float32)
    for r in range(d):
        for m in range(g):
            p[r * g + m, d * m + r] = 1.0
    return jnp.asarray(p, BF16), jnp.asarray(p.T, BF16)


def _lse_expand_table():
    e = np.zeros((ATTN_PAIRS, 2 * ATTN_BLK, 2 * ATTN_DH), np.float32)
    for hp in range(ATTN_PAIRS):
        for dst in range(2 * ATTN_DH):
            src = (dst // ATTN_DH) * ATTN_DH + 16 * hp
            e[hp, src, dst] = 1.0
            e[hp, ATTN_BLK + src, dst] = 1.0
    return jnp.asarray(e, BF16)


def _attn_consts():
    blk = ATTN_BLK
    qi = lax.broadcasted_iota(jnp.int32, (blk, 2 * blk), 0)
    kj = lax.broadcasted_iota(jnp.int32, (blk, 2 * blk), 1)
    band = (kj >= qi) & (kj <= qi + blk)
    neg = jnp.float32(-jnp.inf)
    bias_mid = jnp.where(band, 0.0, neg)
    bias_first = jnp.where(band & (kj >= blk), 0.0, neg)
    lane = lax.broadcasted_iota(jnp.int32, (blk, 2 * ATTN_DH), 1)
    return bias_mid, bias_first, lane


def _attn_scores(load_q, kcats, bias, low):
    zero = jnp.zeros((), BF16)
    out = []
    for hp in range(ATTN_PAIRS):
        q2 = load_q(hp)
        for half in range(2):
            qh = jnp.where(low if half == 0 else ~low, q2, zero)
            out.append(lax.dot_general(qh, kcats[hp], NT_DIMS, preferred_element_type=F32) + bias)
    return out


def _attn_softmax_pv(scores, vcats, low):
    one = jnp.ones((), BF16)
    lowv = lax.broadcasted_iota(jnp.int32, (2 * ATTN_BLK, 2 * ATTN_DH), 1) < ATTN_DH
    maxes = [jnp.max(s, axis=-1, keepdims=True) for s in scores]
    probs = [jnp.exp(s - m).astype(BF16) for s, m in zip(scores, maxes)]
    res = []
    for hp in range(ATTN_PAIRS):
        ra = _dot(probs[2 * hp], jnp.where(lowv, vcats[hp], one))
        rb = _dot(probs[2 * hp + 1], jnp.where(lowv, one, vcats[hp]))
        den = pltpu.roll(jnp.where(low, rb, ra), ATTN_DH, axis=1)
        out = jnp.where(low, ra, rb) / den
        lse = jnp.where(low, maxes[2 * hp], maxes[2 * hp + 1]) + jnp.log(den)
        res.append((out, lse))
    return res


def _attn_strided_kernel(d, q_ref, k_ref, v_ref, perm_ref, permt_ref, o_ref, lsec_ref,
                         qs, ks, vs, kp, vp, ost, lst):
    blk = ATTN_BLK
    grp = ATTN_GROUP // d
    rrows = ATTN_SUPER // d
    nj = rrows // blk
    ngroups = ATTN_SUPER // ATTN_GROUP
    step = pl.program_id(1)

    @pl.when(step == 0)
    def _():
        kp[...] = jnp.zeros_like(kp)
        vp[...] = jnp.zeros_like(vp)

    perm = perm_ref[...]
    for src, dst in ((q_ref, qs), (k_ref, ks), (v_ref, vs)):
        for g in range(ngroups):
            y = _dot(perm, src[g * ATTN_GROUP:(g + 1) * ATTN_GROUP, :]).astype(BF16)
            for r in range(d):
                dst[r, g * grp:(g + 1) * grp, :] = y[r * grp:(r + 1) * grp, :]

    bias_mid, bias_first, lane = _attn_consts()
    low = lane < ATTN_DH
    lane_grp = (lane % ATTN_DH) // 16
    units = [(r, j) for r in range(d) for j in range(nj)]

    def cat(cur, prev, r, j, hp):
        cols = slice(hp * 2 * ATTN_DH, (hp + 1) * 2 * ATTN_DH)
        before = prev[r, :, cols] if j == 0 else cur[r, (j - 1) * blk:j * blk, cols]
        return jnp.concatenate([before, cur[r, j * blk:(j + 1) * blk, cols]], axis=0)

    def scores(u):
        r, j = units[u]
        bias = jnp.where(step == 0, bias_first, bias_mid) if j == 0 else bias_mid
        kcats = [cat(ks, kp, r, j, hp) for hp in range(ATTN_PAIRS)]
        load_q = lambda hp: qs[r, j * blk:(j + 1) * blk, hp * 2 * ATTN_DH:(hp + 1) * 2 * ATTN_DH]
        return _attn_scores(load_q, kcats, bias, low)

    def finish(u, sc):
        r, j = units[u]
        rows = slice(j * blk, (j + 1) * blk)
        res = _attn_softmax_pv(sc, [cat(vs, vp, r, j, hp) for hp in range(ATTN_PAIRS)], low)
        comp = res[ATTN_PAIRS - 1][1]
        for hp in range(ATTN_PAIRS - 2, -1, -1):
            comp = jnp.where(lane_grp == hp, res[hp][1], comp)
        for hp in range(ATTN_PAIRS):
            ost[r, rows, hp * 2 * ATTN_DH:(hp + 1) * 2 * ATTN_DH] = res[hp][0].astype(BF16)
        hi = comp.astype(BF16)
        lst[r, rows, 0:blk] = hi
        lst[r, rows, blk:2 * blk] = (comp - hi.astype(F32)).astype(BF16)

    nxt = scores(0)
    for u in range(len(units)):
        cur = nxt
        if u + 1 < len(units):
            nxt = scores(u + 1)
        finish(u, cur)

    permt = permt_ref[...]
    for g in range(ngroups):
        rows = slice(g * ATTN_GROUP, (g + 1) * ATTN_GROUP)
        og = jnp.concatenate([ost[r, g * grp:(g + 1) * grp, :] for r in range(d)], axis=0)
        o_ref[rows, :] = _dot(permt, og).astype(BF16)
        lg = jnp.concatenate([lst[r, g * grp:(g + 1) * grp, :] for r in range(d)], axis=0)
        lsec_ref[rows, :] = _dot(permt, lg).astype(BF16)

    kp[...] = ks[:, rrows - blk:rrows, :]
    vp[...] = vs[:, rrows - blk:rrows, :]


def _attn_merge_kernel(q_ref, k_ref, kp_ref, v_ref, vp_ref, o16_ref, l16_ref, o4_ref, l4_ref,
                       ee_ref, o_ref):
    blk = ATTN_BLK
    nq = ATTN_D1_ROWS // blk
    step = pl.program_id(1)
    bias_mid, bias_first, lane = _attn_consts()
    low = lane < ATTN_DH

    def cat(cur, prev, j, hp):
        cols = slice(hp * 2 * ATTN_DH, (hp + 1) * 2 * ATTN_DH)
        before = prev[:, cols] if j == 0 else cur[(j - 1) * blk:j * blk, cols]
        return jnp.concatenate([before, cur[j * blk:(j + 1) * blk, cols]], axis=0)

    def scores(j):
        bias = jnp.where(step == 0, bias_first, bias_mid) if j == 0 else bias_mid
        kcats = [cat(k_ref, kp_ref, j, hp) for hp in range(ATTN_PAIRS)]
        load_q = lambda hp: q_ref[j * blk:(j + 1) * blk, hp * 2 * ATTN_DH:(hp + 1) * 2 * ATTN_DH]
        return _attn_scores(load_q, kcats, bias, low)

    def finish(j, sc):
        rows = slice(j * blk, (j + 1) * blk)
        res = _attn_softmax_pv(sc, [cat(v_ref, vp_ref, j, hp) for hp in range(ATTN_PAIRS)], low)
        l16c = l16_ref[rows, :]
        l4c = l4_ref[rows, :]
        for hp in range(ATTN_PAIRS):
            cols = slice(hp * 2 * ATTN_DH, (hp + 1) * 2 * ATTN_DH)
            o1, lse1 = res[hp]
            lse16 = _dot(l16c, ee_ref[hp])
            lse4 = _dot(l4c, ee_ref[hp])
            mx = jnp.maximum(jnp.maximum(lse16, lse4), lse1)
            w16 = jnp.exp(lse16 - mx)
            w4 = jnp.exp(lse4 - mx)
            w1 = jnp.exp(lse1 - mx)
            num = (w16 * o16_ref[rows, cols].astype(F32) + w4 * o4_ref[rows, cols].astype(F32)
                   + w1 * o1)
            o_ref[rows, cols] = (num / (w16 + w4 + w1)).astype(o_ref.dtype)

    nxt = scores(0)
    for j in range(nq):
        cur = nxt
        if j + 1 < nq:
            nxt = scores(j + 1)
        finish(j, cur)


def _dilated_attention(q, k, v):
    b, s, _ = q.shape
    blk = ATTN_BLK
    sds = jax.ShapeDtypeStruct
    params = pltpu.CompilerParams(dimension_semantics=("arbitrary", "arbitrary"),
                                  vmem_limit_bytes=VMEM_LIMIT)
    partial_outs = []
    for d in (16, 4):
        perm, permt = _attn_tables(d)
        rrows = ATTN_SUPER // d
        tokens = lambda w: pl.BlockSpec((None, ATTN_SUPER, w), lambda bi, i: (bi, i, 0))
        table = pl.BlockSpec((ATTN_GROUP, ATTN_GROUP), lambda bi, i: (0, 0))
        residue_major = lambda rows, w: pltpu.VMEM((d, rows, w), BF16)
        partial_outs += pl.pallas_call(
            functools.partial(_attn_strided_kernel, d),
            grid=(b, s // ATTN_SUPER),
            in_specs=[tokens(ATTN_WIDTH)] * 3 + [table, table],
            out_specs=[tokens(ATTN_WIDTH), tokens(2 * blk)],
            out_shape=[sds((b, s, ATTN_WIDTH), BF16), sds((b, s, 2 * blk), BF16)],
            scratch_shapes=[residue_major(rrows, ATTN_WIDTH)] * 3
                           + [residue_major(blk, ATTN_WIDTH)] * 2
                           + [residue_major(rrows, ATTN_WIDTH), residue_major(rrows, 2 * blk)],
            compiler_params=params,
            name=f"dilated_attn_d{d}",
        )(q, k, v, perm, permt)

    nq = ATTN_D1_ROWS // blk
    tokens = lambda w: pl.BlockSpec((None, ATTN_D1_ROWS, w), lambda bi, i: (bi, i, 0))
    prev = pl.BlockSpec((None, blk, ATTN_WIDTH), lambda bi, i: (bi, jnp.maximum(i * nq - 1, 0), 0))
    ee = _lse_expand_table()
    return pl.pallas_call(
        _attn_merge_kernel,
        grid=(b, s // ATTN_D1_ROWS),
        in_specs=[tokens(ATTN_WIDTH), tokens(ATTN_WIDTH), prev, tokens(ATTN_WIDTH), prev,
                  tokens(ATTN_WIDTH), tokens(2 * blk), tokens(ATTN_WIDTH), tokens(2 * blk),
                  pl.BlockSpec(ee.shape, lambda bi, i: (0, 0, 0))],
        out_specs=tokens(ATTN_WIDTH),
        out_shape=sds((b, s, ATTN_WIDTH), BF16),
        compiler_params=params,
        name="dilated_attn_d1_merge",
    )(q, k, k, v, v, *partial_outs, ee)


def _layer_norm(y, g, b):
    mu = jnp.mean(y, axis=-1, keepdims=True)
    yc = y - mu
    var = jnp.mean(yc * yc, axis=-1, keepdims=True)
    return yc * lax.rsqrt(var + LN_EPS) * g + b


def _out_mlp_kernel(x_ref, orec_ref, oatt_ref, wout_ref, g1_ref, b1_ref, w1_ref, w2_ref,
                    g2_ref, b2_ref, y_ref):
    sub = MLP_SUBTILE
    tiles = [slice(r, r + sub) for r in range(0, x_ref.shape[0], sub)]
    mixes = [_dot(orec_ref[rows, :], wout_ref[0:HGRN_WIDTH, :])
             + _dot(oatt_ref[rows, :], wout_ref[HGRN_WIDTH:D_MODEL, :]) for rows in tiles]
    for rows, mix in zip(tiles, mixes):
        x1 = _layer_norm(DEEPNORM_ALPHA * x_ref[rows, :] + mix, g1_ref[...], b1_ref[...])
        x1b = x1.astype(BF16)
        acc = jnp.zeros(x1.shape, F32)
        for c in range(D_FF // MLP_FF_CHUNK):
            cols = slice(c * MLP_FF_CHUNK, (c + 1) * MLP_FF_CHUNK)
            h = jnp.maximum(_dot(x1b, w1_ref[:, cols]), 0.0)
            acc = acc + _dot((h * h).astype(BF16), w2_ref[cols, :])
        y_ref[rows, :] = _layer_norm(DEEPNORM_ALPHA * x1 + acc, g2_ref[...], b2_ref[...])


def _out_mlp(x2d, o_rec, o_att, w_out, g1, b1, w1, w2, g2, b2):
    t = x2d.shape[0]
    tm = TOKEN_TILE
    row = lambda i: (i, 0)
    const = lambda shape: pl.BlockSpec(shape, lambda i: (0, 0), pipeline_mode=pl.Buffered(1))
    return pl.pallas_call(
        _out_mlp_kernel,
        grid=(t // tm,),
        in_specs=[pl.BlockSpec((tm, D_MODEL), row),
                  pl.BlockSpec((tm, HGRN_WIDTH), row),
                  pl.BlockSpec((tm, ATTN_WIDTH), row),
                  const((D_MODEL, D_MODEL)), const((1, D_MODEL)), const((1, D_MODEL)),
                  const((D_MODEL, D_FF)), const((D_FF, D_MODEL)),
                  const((1, D_MODEL)), const((1, D_MODEL))],
        out_specs=pl.BlockSpec((tm, D_MODEL), row),
        out_shape=jax.ShapeDtypeStruct((t, D_MODEL), F32),
        compiler_params=pltpu.CompilerParams(dimension_semantics=("arbitrary",),
                                             vmem_limit_bytes=VMEM_LIMIT),
        name="out_proj_mlp",
    )(x2d, o_rec, o_att, w_out, g1, b1, w1, w2, g2, b2)


def kernel(x, w_in, w_out, lower_bounds, hgrn_norm_w, ln1_g, ln1_b, w_ff1, w_ff2, ln2_g, ln2_b):
    b, s, _ = x.shape
    t = b * s
    x2d = x.reshape(t, D_MODEL)
    row = lambda a, l: a[l].reshape(1, -1).astype(F32)
    for l in range(DEPTH):
        hg, q, k, v = _in_proj(x2d, w_in[l].astype(BF16))
        o_rec = _hgrn(hg.reshape(b, s, 4 * HGRN_WIDTH), lower_bounds.astype(F32),
                      row(hgrn_norm_w, l), l)
        o_att = _dilated_attention(*(a.reshape(b, s, ATTN_WIDTH) for a in (q, k, v)))
        x2d = _out_mlp(x2d, o_rec.reshape(t, HGRN_WIDTH), o_att.reshape(t, ATTN_WIDTH),
                       w_out[l].astype(BF16), row(ln1_g, l), row(ln1_b, l),
                       w_ff1[l].astype(BF16), w_ff2[l].astype(BF16), row(ln2_g, l), row(ln2_b, l))
    return x2d.reshape(b, s, D_MODEL)
```

```python
import functools

import numpy as np
import jax
import jax.numpy as jnp
from jax import lax
from jax.experimental import pallas as pl
from jax.experimental.pallas import tpu as pltpu

F32 = jnp.float32
BF16 = jnp.bfloat16

D_MODEL = 1024
HGRN_WIDTH = 512
HGRN_HEADS = 4
HGRN_DK = 128
ATTN_WIDTH = 512
ATTN_HEADS = 8
ATTN_DH = 64
ATTN_PAIRS = ATTN_HEADS // 2
ATTN_BLK = 128
ATTN_SUPER = 2048
ATTN_GROUP = 256
ATTN_D1_ROWS = 1024
D_FF = 4 * D_MODEL
IN_COLS = 4 * HGRN_WIDTH + 3 * ATTN_WIDTH
DEPTH = 2
DEEPNORM_ALPHA = (2 * DEPTH) ** 0.25
LN_EPS = 1e-5
RMS_EPS = 1e-6
LOG2_E = 1.4426950408889634

HGRN_CHUNK = 64
HGRN_LEVELS = 6
HGRN_GROUP_CHUNKS = 4
HGRN_ROWS = 512
TOKEN_TILE = 512
MLP_TILE = 1024
MLP_SUBTILE = 256
MLP_FF_CHUNK = 1024
VMEM_LIMIT = 48 * 1024 * 1024
MLP_VMEM_LIMIT = 56 * 1024 * 1024

NT_DIMS = (((1,), (1,)), ((), ()))
TN_DIMS = (((0,), (0,)), ((), ()))


def _dot(a, b):
    return jnp.dot(a, b, preferred_element_type=F32)


def _sigmoid(x):
    return 0.5 * jnp.tanh(0.5 * x) + 0.5


def _in_proj_kernel(x_ref, w_ref, hg_ref, q_ref, k_ref, v_ref):
    xb = x_ref[...].astype(BF16)
    for c in range(4):
        cols = slice(c * HGRN_WIDTH, (c + 1) * HGRN_WIDTH)
        hg_ref[:, cols] = _dot(xb, w_ref[:, cols])
    base = 4 * HGRN_WIDTH
    q = _dot(xb, w_ref[:, base:base + ATTN_WIDTH]) * (ATTN_DH ** -0.5)
    q_ref[...] = q.astype(BF16)
    k_ref[...] = _dot(xb, w_ref[:, base + ATTN_WIDTH:base + 2 * ATTN_WIDTH]).astype(BF16)
    v_ref[...] = _dot(xb, w_ref[:, base + 2 * ATTN_WIDTH:base + 3 * ATTN_WIDTH]).astype(BF16)


def _in_proj(x2d, w_in_bf16):
    t = x2d.shape[0]
    tm = TOKEN_TILE
    row = lambda i: (i, 0)
    return pl.pallas_call(
        _in_proj_kernel,
        grid=(t // tm,),
        in_specs=[pl.BlockSpec((tm, D_MODEL), row),
                  pl.BlockSpec((D_MODEL, IN_COLS), lambda i: (0, 0))],
        out_specs=[pl.BlockSpec((tm, 4 * HGRN_WIDTH), row),
                   pl.BlockSpec((tm, ATTN_WIDTH), row),
                   pl.BlockSpec((tm, ATTN_WIDTH), row),
                   pl.BlockSpec((tm, ATTN_WIDTH), row)],
        out_shape=[jax.ShapeDtypeStruct((t, 4 * HGRN_WIDTH), F32),
                   jax.ShapeDtypeStruct((t, ATTN_WIDTH), BF16),
                   jax.ShapeDtypeStruct((t, ATTN_WIDTH), BF16),
                   jax.ShapeDtypeStruct((t, ATTN_WIDTH), BF16)],
        compiler_params=pltpu.CompilerParams(dimension_semantics=("arbitrary",),
                                             vmem_limit_bytes=VMEM_LIMIT),
        name="in_proj",
    )(x2d, w_in_bf16)


def _hgrn_tables():
    c, nl = HGRN_CHUNK, HGRN_LEVELS
    t = np.arange(c)[:, None]
    u = np.arange(c)[None, :]
    tabs = []
    for l in range(nl):
        h = 1 << l
        upper = ((t >> l) & 1) == 1
        r = ((t >> (l + 1)) << (l + 1)) + h - 1
        tabs.append(np.where(upper, (u > r) & (u <= t), (u > t) & (u <= r)))
    tabs.append(u <= t)
    tabs.append(u > t)
    w = np.concatenate(tabs, axis=0).astype(np.float32)
    w3 = np.concatenate([w, w, w], axis=1)
    x = t ^ u
    lvl = np.where(u < t, np.floor(np.log2(np.maximum(x, 1))), np.where(u == t, nl, nl + 1))
    return jnp.asarray(w3, BF16), jnp.asarray(lvl, jnp.int32)


def _hgrn_kernel(layer, q_ref, f_ref, i_ref, g_ref, lbraw_ref, nw_ref, w3_ref, lvl_ref,
                 o_ref, state_ref):
    c, nl, nh, dk = HGRN_CHUNK, HGRN_LEVELS, HGRN_HEADS, HGRN_DK
    head = lambda a, h: a[:, h * dk:(h + 1) * dk]

    @pl.when(pl.program_id(1) == 0)
    def _():
        state_ref[...] = jnp.zeros_like(state_ref)

    raw = lbraw_ref[...]
    e = jnp.exp(raw - jnp.max(raw, axis=0, keepdims=True))
    sm = e / jnp.sum(e, axis=0, keepdims=True)
    first = sm[0:1, :]
    cum = first
    for j in range(1, layer + 1):
        cum = cum + sm[j:j + 1, :]
    lb = cum - first
    log_lb = jnp.log(lb)
    log_1mlb = jnp.log1p(-lb)
    one_m_lb = 1.0 - lb

    nw = nw_ref[...]
    w3 = w3_ref[...]
    lvl = lvl_ref[...]
    row = lax.broadcasted_iota(jnp.int32, (c, HGRN_WIDTH), 0)
    upper = [((row >> l) & 1) == 1 for l in range(nl)]

    def group(it):
        chunks = []
        for ch in range(HGRN_GROUP_CHUNKS):
            start = (it * HGRN_GROUP_CHUNKS + ch) * c
            rows = slice(start, start + c)
            qr = q_ref[rows, :]
            z = f_ref[rows, :]
            v = i_ref[rows, :].astype(BF16)
            q = qr * _sigmoid(qr)
            log_sig = jnp.minimum(z, 0.0) - jnp.log(1.0 + jnp.exp(-jnp.abs(z)))
            a = log_1mlb + log_sig
            lf = jnp.maximum(log_lb, a) + jnp.log(1.0 + jnp.exp(-jnp.abs(log_lb - a)))
            k = one_m_lb * _sigmoid(-z)
            lf2 = lf * LOG2_E
            hi = lf2.astype(BF16)
            r1 = lf2 - hi.astype(F32)
            mid = r1.astype(BF16)
            lo = (r1 - mid.astype(F32)).astype(BF16)
            ex = jnp.exp2(_dot(w3, jnp.concatenate([hi, mid, lo], axis=0)))
            chunks.append((rows, q, k, v, ex))

        amats = []
        for rows, q, k, v, ex in chunks:
            xs = [(jnp.where(upper[l], q, k) * ex[l * c:(l + 1) * c]).astype(BF16) for l in range(nl)]
            qb = q.astype(BF16)
            kb = k.astype(BF16)
            per_head = []
            for h in range(nh):
                amat = lax.dot_general(head(qb, h), head(kb, h), NT_DIMS, preferred_element_type=F32)
                amat = jnp.where(lvl == nl, amat, 0.0)
                for l in range(nl):
                    x = head(xs[l], h)
                    s = lax.dot_general(x, x, NT_DIMS, preferred_element_type=F32)
                    amat = jnp.where(lvl == l, s, amat)
                per_head.append(amat.astype(BF16))
            amats.append(per_head)

        upds = []
        for rows, q, k, v, ex in chunks:
            kd = (k * ex[(nl + 1) * c:(nl + 2) * c]).astype(BF16)
            upds.append([lax.dot_general(head(v, h), head(kd, h), TN_DIMS, preferred_element_type=F32)
                         for h in range(nh)])

        before = []
        for h in range(nh):
            st = state_ref[h]
            seen = []
            for ci, (rows, q, k, v, ex) in enumerate(chunks):
                seen.append(st.astype(BF16))
                st = st * head(ex[(nl + 1) * c - 1:(nl + 1) * c], h) + upds[ci][h]
            state_ref[h] = st
            before.append(seen)

        for ci, (rows, q, k, v, ex) in enumerate(chunks):
            qs = (q * ex[nl * c:(nl + 1) * c]).astype(BF16)
            outs = []
            for h in range(nh):
                o = _dot(amats[ci][h], head(v, h))
                o = o + lax.dot_general(head(qs, h), before[h][ci], NT_DIMS, preferred_element_type=F32)
                outs.append(o * lax.rsqrt(jnp.mean(o * o, axis=-1, keepdims=True) + RMS_EPS))
            o = jnp.concatenate(outs, axis=1) * nw
            gate = g_ref[rows, :]
            o_ref[rows, :] = (o * _sigmoid(gate)).astype(o_ref.dtype)

    for it in range(q_ref.shape[0] // (c * HGRN_GROUP_CHUNKS)):
        group(it)


def _hgrn(hg3d, lower_bounds, norm_w_row, layer):
    b, s, _ = hg3d.shape
    rows = HGRN_ROWS
    w3, lvl = _hgrn_tables()
    part = lambda p: pl.BlockSpec((None, rows, HGRN_WIDTH), lambda bi, i: (bi, i, p))
    full = lambda shape: pl.BlockSpec(shape, lambda bi, i: (0, 0))
    return pl.pallas_call(
        functools.partial(_hgrn_kernel, layer),
        grid=(b, s // rows),
        in_specs=[part(0), part(1), part(2), part(3),
                  full((DEPTH, HGRN_WIDTH)), full((1, HGRN_WIDTH)), full(w3.shape), full(lvl.shape)],
        out_specs=pl.BlockSpec((None, rows, HGRN_WIDTH), lambda bi, i: (bi, i, 0)),
        out_shape=jax.ShapeDtypeStruct((b, s, HGRN_WIDTH), BF16),
        scratch_shapes=[pltpu.VMEM((HGRN_HEADS, HGRN_DK, HGRN_DK), F32)],
        compiler_params=pltpu.CompilerParams(dimension_semantics=("arbitrary", "arbitrary"),
                                             vmem_limit_bytes=VMEM_LIMIT),
        name="hgrn2",
    )(hg3d, hg3d, hg3d, hg3d, lower_bounds, norm_w_row, w3, lvl)


def _attn_tables(d):
    g = ATTN_GROUP // d
    p = np.zeros((ATTN_GROUP, ATTN_GROUP), np.float32)
    for r in range(d):
        for m in range(g):
            p[r * g + m, d * m + r] = 1.0
    return jnp.asarray(p, BF16), jnp.asarray(p.T, BF16)


def _lse_expand_table():
    e = np.zeros((ATTN_PAIRS, 2 * ATTN_BLK, 2 * ATTN_DH), np.float32)
    for hp in range(ATTN_PAIRS):
        for dst in range(2 * ATTN_DH):
            src = (dst // ATTN_DH) * ATTN_DH + 16 * hp
            e[hp, src, dst] = 1.0
            e[hp, ATTN_BLK + src, dst] = 1.0
    return jnp.asarray(e, BF16)


def _attn_consts():
    blk = ATTN_BLK
    qi = lax.broadcasted_iota(jnp.int32, (blk, 2 * blk), 0)
    kj = lax.broadcasted_iota(jnp.int32, (blk, 2 * blk), 1)
    band = (kj >= qi) & (kj <= qi + blk)
    neg = jnp.float32(-jnp.inf)
    bias_mid = jnp.where(band, 0.0, neg)
    bias_first = jnp.where(band & (kj >= blk), 0.0, neg)
    lane = lax.broadcasted_iota(jnp.int32, (blk, 2 * ATTN_DH), 1)
    return bias_mid, bias_first, lane


def _attn_scores(load_q, kcats, bias, low):
    zero = jnp.zeros((), BF16)
    out = []
    for hp in range(ATTN_PAIRS):
        q2 = load_q(hp)
        for half in range(2):
            qh = jnp.where(low if half == 0 else ~low, q2, zero)
            out.append(lax.dot_general(qh, kcats[hp], NT_DIMS, preferred_element_type=F32) + bias)
    return out


def _attn_softmax_pv(scores, vcats, low):
    one = jnp.ones((), BF16)
    lowv = lax.broadcasted_iota(jnp.int32, (2 * ATTN_BLK, 2 * ATTN_DH), 1) < ATTN_DH
    maxes = [jnp.max(s, axis=-1, keepdims=True) for s in scores]
    probs = [jnp.exp(s - m).astype(BF16) for s, m in zip(scores, maxes)]
    res = []
    for hp in range(ATTN_PAIRS):
        ra = _dot(probs[2 * hp], jnp.where(lowv, vcats[hp], one))
        rb = _dot(probs[2 * hp + 1], jnp.where(lowv, one, vcats[hp]))
        den = pltpu.roll(jnp.where(low, rb, ra), ATTN_DH, axis=1)
        out = jnp.where(low, ra, rb) / den
        lse = jnp.where(low, maxes[2 * hp], maxes[2 * hp + 1]) + jnp.log(den)
        res.append((out, lse))
    return res


def _attn_strided_kernel(d, q_ref, k_ref, v_ref, perm_ref, permt_ref, o_ref, lsec_ref,
                         qs, ks, vs, kp, vp, ost, lst):
    blk = ATTN_BLK
    grp = ATTN_GROUP // d
    rrows = ATTN_SUPER // d
    nj = rrows // blk
    ngroups = ATTN_SUPER // ATTN_GROUP
    step = pl.program_id(1)

    @pl.when(step == 0)
    def _():
        kp[...] = jnp.zeros_like(kp)
        vp[...] = jnp.zeros_like(vp)

    perm = perm_ref[...]

    def reorder(src, dst):
        for g in range(ngroups):
            y = _dot(perm, src[g * ATTN_GROUP:(g + 1) * ATTN_GROUP, :]).astype(BF16)
            for r in range(d):
                dst[r, g * grp:(g + 1) * grp, :] = y[r * grp:(r + 1) * grp, :]

    reorder(q_ref, qs)
    reorder(k_ref, ks)

    bias_mid, bias_first, lane = _attn_consts()
    low = lane < ATTN_DH
    lane_grp = (lane % ATTN_DH) // 16
    units = [(r, j) for r in range(d) for j in range(nj)]

    def cat(cur, prev, r, j, hp):
        cols = slice(hp * 2 * ATTN_DH, (hp + 1) * 2 * ATTN_DH)
        before = prev[r, :, cols] if j == 0 else cur[r, (j - 1) * blk:j * blk, cols]
        return jnp.concatenate([before, cur[r, j * blk:(j + 1) * blk, cols]], axis=0)

    def scores(u):
        r, j = units[u]
        bias = jnp.where(step == 0, bias_first, bias_mid) if j == 0 else bias_mid
        kcats = [cat(ks, kp, r, j, hp) for hp in range(ATTN_PAIRS)]
        load_q = lambda hp: qs[r, j * blk:(j + 1) * blk, hp * 2 * ATTN_DH:(hp + 1) * 2 * ATTN_DH]
        return _attn_scores(load_q, kcats, bias, low)

    def finish(u, sc):
        r, j = units[u]
        rows = slice(j * blk, (j + 1) * blk)
        res = _attn_softmax_pv(sc, [cat(vs, vp, r, j, hp) for hp in range(ATTN_PAIRS)], low)
        comp = res[ATTN_PAIRS - 1][1]
        for hp in range(ATTN_PAIRS - 2, -1, -1):
            comp = jnp.where(lane_grp == hp, res[hp][1], comp)
        for hp in range(ATTN_PAIRS):
            ost[r, rows, hp * 2 * ATTN_DH:(hp + 1) * 2 * ATTN_DH] = res[hp][0].astype(BF16)
        hi = comp.astype(BF16)
        lst[r, rows, 0:blk] = hi
        lst[r, rows, blk:2 * blk] = (comp - hi.astype(F32)).astype(BF16)

    nxt = scores(0)
    reorder(v_ref, vs)
    for u in range(len(units)):
        cur = nxt
        if u + 1 < len(units):
            nxt = scores(u + 1)
        finish(u, cur)

    permt = permt_ref[...]
    for g in range(ngroups):
        rows = slice(g * ATTN_GROUP, (g + 1) * ATTN_GROUP)
        og = jnp.concatenate([ost[r, g * grp:(g + 1) * grp, :] for r in range(d)], axis=0)
        o_ref[rows, :] = _dot(permt, og).astype(BF16)
        lg = jnp.concatenate([lst[r, g * grp:(g + 1) * grp, :] for r in range(d)], axis=0)
        lsec_ref[rows, :] = _dot(permt, lg).astype(BF16)

    kp[...] = ks[:, rrows - blk:rrows, :]
    vp[...] = vs[:, rrows - blk:rrows, :]


def _attn_merge_kernel(q_ref, k_ref, kp_ref, v_ref, vp_ref, o16_ref, l16_ref, o4_ref, l4_ref,
                       ee_ref, o_ref):
    blk = ATTN_BLK
    nq = ATTN_D1_ROWS // blk
    step = pl.program_id(1)
    bias_mid, bias_first, lane = _attn_consts()
    low = lane < ATTN_DH

    def cat(cur, prev, j, hp):
        cols = slice(hp * 2 * ATTN_DH, (hp + 1) * 2 * ATTN_DH)
        before = prev[:, cols] if j == 0 else cur[(j - 1) * blk:j * blk, cols]
        return jnp.concatenate([before, cur[j * blk:(j + 1) * blk, cols]], axis=0)

    def scores(j):
        bias = jnp.where(step == 0, bias_first, bias_mid) if j == 0 else bias_mid
        kcats = [cat(k_ref, kp_ref, j, hp) for hp in range(ATTN_PAIRS)]
        load_q = lambda hp: q_ref[j * blk:(j + 1) * blk, hp * 2 * ATTN_DH:(hp + 1) * 2 * ATTN_DH]
        return _attn_scores(load_q, kcats, bias, low)

    def finish(j, sc):
        rows = slice(j * blk, (j + 1) * blk)
        res = _attn_softmax_pv(sc, [cat(v_ref, vp_ref, j, hp) for hp in range(ATTN_PAIRS)], low)
        l16c = l16_ref[rows, :]
        l4c = l4_ref[rows, :]
        for hp in range(ATTN_PAIRS):
            cols = slice(hp * 2 * ATTN_DH, (hp + 1) * 2 * ATTN_DH)
            o1, lse1 = res[hp]
            lse16 = _dot(l16c, ee_ref[hp])
            lse4 = _dot(l4c, ee_ref[hp])
            mx = jnp.maximum(jnp.maximum(lse16, lse4), lse1)
            w16 = jnp.exp(lse16 - mx)
            w4 = jnp.exp(lse4 - mx)
            w1 = jnp.exp(lse1 - mx)
            num = (w16 * o16_ref[rows, cols].astype(F32) + w4 * o4_ref[rows, cols].astype(F32)
                   + w1 * o1)
            o_ref[rows, cols] = (num / (w16 + w4 + w1)).astype(o_ref.dtype)

    nxt = scores(0)
    for j in range(nq):
        cur = nxt
        if j + 1 < nq:
            nxt = scores(j + 1)
        finish(j, cur)


def _dilated_attention(q, k, v):
    b, s, _ = q.shape
    blk = ATTN_BLK
    sds = jax.ShapeDtypeStruct
    params = pltpu.CompilerParams(dimension_semantics=("arbitrary", "arbitrary"),
                                  vmem_limit_bytes=VMEM_LIMIT)
    partial_outs = []
    for d in (16, 4):
        perm, permt = _attn_tables(d)
        rrows = ATTN_SUPER // d
        tokens = lambda w: pl.BlockSpec((None, ATTN_SUPER, w), lambda bi, i: (bi, i, 0))
        table = pl.BlockSpec((ATTN_GROUP, ATTN_GROUP), lambda bi, i: (0, 0))
        residue_major = lambda rows, w: pltpu.VMEM((d, rows, w), BF16)
        partial_outs += pl.pallas_call(
            functools.partial(_attn_strided_kernel, d),
            grid=(b, s // ATTN_SUPER),
            in_specs=[tokens(ATTN_WIDTH)] * 3 + [table, table],
            out_specs=[tokens(ATTN_WIDTH), tokens(2 * blk)],
            out_shape=[sds((b, s, ATTN_WIDTH), BF16), sds((b, s, 2 * blk), BF16)],
            scratch_shapes=[residue_major(rrows, ATTN_WIDTH)] * 3
                           + [residue_major(blk, ATTN_WIDTH)] * 2
                           + [residue_major(rrows, ATTN_WIDTH), residue_major(rrows, 2 * blk)],
            compiler_params=params,
            name=f"dilated_attn_d{d}",
        )(q, k, v, perm, permt)

    nq = ATTN_D1_ROWS // blk
    tokens = lambda w: pl.BlockSpec((None, ATTN_D1_ROWS, w), lambda bi, i: (bi, i, 0))
    prev = pl.BlockSpec((None, blk, ATTN_WIDTH), lambda bi, i: (bi, jnp.maximum(i * nq - 1, 0), 0))
    ee = _lse_expand_table()
    return pl.pallas_call(
        _attn_merge_kernel,
        grid=(b, s // ATTN_D1_ROWS),
        in_specs=[tokens(ATTN_WIDTH), tokens(ATTN_WIDTH), prev, tokens(ATTN_WIDTH), prev,
                  tokens(ATTN_WIDTH), tokens(2 * blk), tokens(ATTN_WIDTH), tokens(2 * blk),
                  pl.BlockSpec(ee.shape, lambda bi, i: (0, 0, 0))],
        out_specs=tokens(ATTN_WIDTH),
        out_shape=sds((b, s, ATTN_WIDTH), BF16),
        compiler_params=params,
        name="dilated_attn_d1_merge",
    )(q, k, k, v, v, *partial_outs, ee)


def _layer_norm(y, g, b):
    mu = jnp.mean(y, axis=-1, keepdims=True)
    yc = y - mu
    var = jnp.mean(yc * yc, axis=-1, keepdims=True)
    return yc * lax.rsqrt(var + LN_EPS) * g + b


def _out_mlp_kernel(x_ref, orec_ref, oatt_ref, wout_ref, g1_ref, b1_ref, w1_ref, w2_ref,
                    g2_ref, b2_ref, y_ref):
    sub = MLP_SUBTILE
    tiles = [slice(r, r + sub) for r in range(0, x_ref.shape[0], sub)]
    mixes = [_dot(orec_ref[rows, :], wout_ref[0:HGRN_WIDTH, :])
             + _dot(oatt_ref[rows, :], wout_ref[HGRN_WIDTH:D_MODEL, :]) for rows in tiles]
    for rows, mix in zip(tiles, mixes):
        x1 = _layer_norm(DEEPNORM_ALPHA * x_ref[rows, :] + mix, g1_ref[...], b1_ref[...])
        x1b = x1.astype(BF16)
        acc = jnp.zeros(x1.shape, F32)
        for c in range(D_FF // MLP_FF_CHUNK):
            cols = slice(c * MLP_FF_CHUNK, (c + 1) * MLP_FF_CHUNK)
            h = jnp.maximum(_dot(x1b, w1_ref[:, cols]), 0.0)
            acc = acc + _dot((h * h).astype(BF16), w2_ref[cols, :])
        y_ref[rows, :] = _layer_norm(DEEPNORM_ALPHA * x1 + acc, g2_ref[...], b2_ref[...])


def _out_mlp(x2d, o_rec, o_att, w_out, g1, b1, w1, w2, g2, b2):
    t = x2d.shape[0]
    tm = MLP_TILE
    row = lambda i: (i, 0)
    const = lambda shape: pl.BlockSpec(shape, lambda i: (0, 0), pipeline_mode=pl.Buffered(1))
    return pl.pallas_call(
        _out_mlp_kernel,
        grid=(t // tm,),
        in_specs=[pl.BlockSpec((tm, D_MODEL), row),
                  pl.BlockSpec((tm, HGRN_WIDTH), row),
                  pl.BlockSpec((tm, ATTN_WIDTH), row),
                  const((D_MODEL, D_MODEL)), const((1, D_MODEL)), const((1, D_MODEL)),
                  const((D_MODEL, D_FF)), const((D_FF, D_MODEL)),
                  const((1, D_MODEL)), const((1, D_MODEL))],
        out_specs=pl.BlockSpec((tm, D_MODEL), row),
        out_shape=jax.ShapeDtypeStruct((t, D_MODEL), F32),
        compiler_params=pltpu.CompilerParams(dimension_semantics=("arbitrary",),
                                             vmem_limit_bytes=MLP_VMEM_LIMIT),
        name="out_proj_mlp",
    )(x2d, o_rec, o_att, w_out, g1, b1, w1, w2, g2, b2)


def kernel(x, w_in, w_out, lower_bounds, hgrn_norm_w, ln1_g, ln1_b, w_ff1, w_ff2, ln2_g, ln2_b):
    b, s, _ = x.shape
    t = b * s
    x2d = x.reshape(t, D_MODEL)
    row = lambda a, l: a[l].reshape(1, -1).astype(F32)
    for l in range(DEPTH):
        hg, q, k, v = _in_proj(x2d, w_in[l].astype(BF16))
        o_rec = _hgrn(hg.reshape(b, s, 4 * HGRN_WIDTH), lower_bounds.astype(F32),
                      row(hgrn_norm_w, l), l)
        o_att = _dilated_attention(*(a.reshape(b, s, ATTN_WIDTH) for a in (q, k, v)))
        x2d = _out_mlp(x2d, o_rec.reshape(t, HGRN_WIDTH), o_att.reshape(t, ATTN_WIDTH),
                       w_out[l].astype(BF16), row(ln1_g, l), row(ln1_b, l),
                       w_ff1[l].astype(BF16), w_ff2[l].astype(BF16), row(ln2_g, l), row(ln2_b, l))
    return x2d.reshape(b, s, D_MODEL)
```

```python
import functools

import numpy as np
import jax
import jax.numpy as jnp
from jax import lax
from jax.experimental import pallas as pl
from jax.experimental.pallas import tpu as pltpu

F32 = jnp.float32
BF16 = jnp.bfloat16

D_MODEL = 1024
HGRN_WIDTH = 512
HGRN_HEADS = 4
HGRN_DK = 128
ATTN_WIDTH = 512
ATTN_HEADS = 8
ATTN_DH = 64
ATTN_PAIRS = ATTN_HEADS // 2
ATTN_BLK = 128
ATTN_SUPER = 2048
ATTN_GROUP = 256
ATTN_D1_ROWS = 1024
D_FF = 4 * D_MODEL
IN_COLS = 4 * HGRN_WIDTH + 3 * ATTN_WIDTH
DEPTH = 2
DEEPNORM_ALPHA = (2 * DEPTH) ** 0.25
LN_EPS = 1e-5
RMS_EPS = 1e-6
LOG2_E = 1.4426950408889634

HGRN_CHUNK = 64
HGRN_LEVELS = 6
HGRN_GROUP_CHUNKS = 4
HGRN_ROWS = 512
TOKEN_TILE = 512
MLP_TILE = 1024
MLP_SUBTILE = 256
MLP_FF_CHUNK = 1024
VMEM_LIMIT = 48 * 1024 * 1024
MLP_VMEM_LIMIT = 56 * 1024 * 1024

NT_DIMS = (((1,), (1,)), ((), ()))
TN_DIMS = (((0,), (0,)), ((), ()))


def _dot(a, b):
    return jnp.dot(a, b, preferred_element_type=F32)


def _sigmoid(x):
    return 0.5 * jnp.tanh(0.5 * x) + 0.5


def _hgrn_lower_bound(lbraw_ref, layer):
    raw = lbraw_ref[...]
    e = jnp.exp(raw - jnp.max(raw, axis=0, keepdims=True))
    sm = e / jnp.sum(e, axis=0, keepdims=True)
    first = sm[0:1, :]
    cum = first
    for j in range(1, layer + 1):
        cum = cum + sm[j:j + 1, :]
    return cum - first


def _in_proj_kernel(layer, x_ref, w_ref, lbraw_ref, hg_ref, hk_ref, q_ref, k_ref, v_ref):
    xb = x_ref[...].astype(BF16)
    part = lambda c: slice(c * HGRN_WIDTH, (c + 1) * HGRN_WIDTH)
    hq = _dot(xb, w_ref[:, part(0)])
    hg_ref[:, part(0)] = hq * _sigmoid(hq)
    z = _dot(xb, w_ref[:, part(1)])
    hg_ref[:, part(1)] = z
    lb = _hgrn_lower_bound(lbraw_ref, layer)
    hk_ref[...] = (1.0 - lb) * _sigmoid(-z)
    hg_ref[:, part(2)] = _dot(xb, w_ref[:, part(2)])
    hg_ref[:, part(3)] = _sigmoid(_dot(xb, w_ref[:, part(3)]))
    base = 4 * HGRN_WIDTH
    q = _dot(xb, w_ref[:, base:base + ATTN_WIDTH]) * (ATTN_DH ** -0.5)
    q_ref[...] = q.astype(BF16)
    k_ref[...] = _dot(xb, w_ref[:, base + ATTN_WIDTH:base + 2 * ATTN_WIDTH]).astype(BF16)
    v_ref[...] = _dot(xb, w_ref[:, base + 2 * ATTN_WIDTH:base + 3 * ATTN_WIDTH]).astype(BF16)


def _in_proj(x2d, w_in_bf16, lower_bounds, layer):
    t = x2d.shape[0]
    tm = TOKEN_TILE
    row = lambda i: (i, 0)
    return pl.pallas_call(
        functools.partial(_in_proj_kernel, layer),
        grid=(t // tm,),
        in_specs=[pl.BlockSpec((tm, D_MODEL), row),
                  pl.BlockSpec((D_MODEL, IN_COLS), lambda i: (0, 0)),
                  pl.BlockSpec((DEPTH, HGRN_WIDTH), lambda i: (0, 0))],
        out_specs=[pl.BlockSpec((tm, 4 * HGRN_WIDTH), row),
                   pl.BlockSpec((tm, HGRN_WIDTH), row),
                   pl.BlockSpec((tm, ATTN_WIDTH), row),
                   pl.BlockSpec((tm, ATTN_WIDTH), row),
                   pl.BlockSpec((tm, ATTN_WIDTH), row)],
        out_shape=[jax.ShapeDtypeStruct((t, 4 * HGRN_WIDTH), F32),
                   jax.ShapeDtypeStruct((t, HGRN_WIDTH), F32),
                   jax.ShapeDtypeStruct((t, ATTN_WIDTH), BF16),
                   jax.ShapeDtypeStruct((t, ATTN_WIDTH), BF16),
                   jax.ShapeDtypeStruct((t, ATTN_WIDTH), BF16)],
        compiler_params=pltpu.CompilerParams(dimension_semantics=("arbitrary",),
                                             vmem_limit_bytes=VMEM_LIMIT),
        name="in_proj",
    )(x2d, w_in_bf16, lower_bounds)


def _hgrn_tables():
    c, nl = HGRN_CHUNK, HGRN_LEVELS
    t = np.arange(c)[:, None]
    u = np.arange(c)[None, :]
    tabs = []
    for l in range(nl):
        h = 1 << l
        upper = ((t >> l) & 1) == 1
        r = ((t >> (l + 1)) << (l + 1)) + h - 1
        tabs.append(np.where(upper, (u > r) & (u <= t), (u > t) & (u <= r)))
    tabs.append(u <= t)
    tabs.append(u > t)
    w = np.concatenate(tabs, axis=0).astype(np.float32)
    w3 = np.concatenate([w, w, w], axis=1)
    x = t ^ u
    lvl = np.where(u < t, np.floor(np.log2(np.maximum(x, 1))), np.where(u == t, nl, nl + 1))
    return jnp.asarray(w3, BF16), jnp.asarray(lvl, jnp.int32)


def _hgrn_kernel(layer, q_ref, f_ref, i_ref, g_ref, k_ref, lbraw_ref, nw_ref, w3_ref, lvl_ref,
                 o_ref, state_ref):
    c, nl, nh, dk = HGRN_CHUNK, HGRN_LEVELS, HGRN_HEADS, HGRN_DK
    head = lambda a, h: a[:, h * dk:(h + 1) * dk]

    @pl.when(pl.program_id(1) == 0)
    def _():
        state_ref[...] = jnp.zeros_like(state_ref)

    lb = _hgrn_lower_bound(lbraw_ref, layer)
    log_lb = jnp.log(lb)
    log_1mlb = jnp.log1p(-lb)

    nw = nw_ref[...]
    w3 = w3_ref[...]
    lvl = lvl_ref[...]
    row = lax.broadcasted_iota(jnp.int32, (c, HGRN_WIDTH), 0)
    upper = [((row >> l) & 1) == 1 for l in range(nl)]

    def prepare(it):
        chunks = []
        for ch in range(HGRN_GROUP_CHUNKS):
            start = (it * HGRN_GROUP_CHUNKS + ch) * c
            rows = slice(start, start + c)
            q = q_ref[rows, :]
            k = k_ref[rows, :]
            z = f_ref[rows, :]
            v = i_ref[rows, :].astype(BF16)
            log_sig = jnp.minimum(z, 0.0) - jnp.log(1.0 + jnp.exp(-jnp.abs(z)))
            a = log_1mlb + log_sig
            lf = jnp.maximum(log_lb, a) + jnp.log(1.0 + jnp.exp(-jnp.abs(log_lb - a)))
            lf2 = lf * LOG2_E
            hi = lf2.astype(BF16)
            r1 = lf2 - hi.astype(F32)
            mid = r1.astype(BF16)
            lo = (r1 - mid.astype(F32)).astype(BF16)
            ex = jnp.exp2(_dot(w3, jnp.concatenate([hi, mid, lo], axis=0)))
            chunks.append((rows, q, k, v, ex))
        return chunks

    def process(chunks):
        amats = []
        for rows, q, k, v, ex in chunks:
            xs = [(jnp.where(upper[l], q, k) * ex[l * c:(l + 1) * c]).astype(BF16) for l in range(nl)]
            qb = q.astype(BF16)
            kb = k.astype(BF16)
            per_head = []
            for h in range(nh):
                amat = lax.dot_general(head(qb, h), head(kb, h), NT_DIMS, preferred_element_type=F32)
                amat = jnp.where(lvl == nl, amat, 0.0)
                for l in range(nl):
                    x = head(xs[l], h)
                    s = lax.dot_general(x, x, NT_DIMS, preferred_element_type=F32)
                    amat = jnp.where(lvl == l, s, amat)
                per_head.append(amat.astype(BF16))
            amats.append(per_head)

        upds = []
        for rows, q, k, v, ex in chunks:
            kd = (k * ex[(nl + 1) * c:(nl + 2) * c]).astype(BF16)
            upds.append([lax.dot_general(head(v, h), head(kd, h), TN_DIMS, preferred_element_type=F32)
                         for h in range(nh)])

        before = []
        for h in range(nh):
            st = state_ref[h]
            seen = []
            for ci, (rows, q, k, v, ex) in enumerate(chunks):
                seen.append(st.astype(BF16))
                st = st * head(ex[(nl + 1) * c - 1:(nl + 1) * c], h) + upds[ci][h]
            state_ref[h] = st
            before.append(seen)

        for ci, (rows, q, k, v, ex) in enumerate(chunks):
            qs = (q * ex[nl * c:(nl + 1) * c]).astype(BF16)
            outs = []
            for h in range(nh):
                o = _dot(amats[ci][h], head(v, h))
                o = o + lax.dot_general(head(qs, h), before[h][ci], NT_DIMS, preferred_element_type=F32)
                outs.append(o * lax.rsqrt(jnp.mean(o * o, axis=-1, keepdims=True) + RMS_EPS))
            o = jnp.concatenate(outs, axis=1) * nw
            o_ref[rows, :] = (o * g_ref[rows, :]).astype(o_ref.dtype)

    ngroups = q_ref.shape[0] // (c * HGRN_GROUP_CHUNKS)
    nxt = prepare(0)
    for it in range(ngroups):
        cur = nxt
        if it + 1 < ngroups:
            nxt = prepare(it + 1)
        process(cur)


def _hgrn(hg3d, hk3d, lower_bounds, norm_w_row, layer):
    b, s, _ = hg3d.shape
    rows = HGRN_ROWS
    w3, lvl = _hgrn_tables()
    part = lambda p: pl.BlockSpec((None, rows, HGRN_WIDTH), lambda bi, i: (bi, i, p))
    full = lambda shape: pl.BlockSpec(shape, lambda bi, i: (0, 0))
    return pl.pallas_call(
        functools.partial(_hgrn_kernel, layer),
        grid=(b, s // rows),
        in_specs=[part(0), part(1), part(2), part(3), part(0),
                  full((DEPTH, HGRN_WIDTH)), full((1, HGRN_WIDTH)), full(w3.shape), full(lvl.shape)],
        out_specs=pl.BlockSpec((None, rows, HGRN_WIDTH), lambda bi, i: (bi, i, 0)),
        out_shape=jax.ShapeDtypeStruct((b, s, HGRN_WIDTH), BF16),
        scratch_shapes=[pltpu.VMEM((HGRN_HEADS, HGRN_DK, HGRN_DK), F32)],
        compiler_params=pltpu.CompilerParams(dimension_semantics=("arbitrary", "arbitrary"),
                                             vmem_limit_bytes=VMEM_LIMIT),
        name="hgrn2",
    )(hg3d, hg3d, hg3d, hg3d, hk3d, lower_bounds, norm_w_row, w3, lvl)


def _attn_tables(d):
    g = ATTN_GROUP // d
    p = np.zeros((ATTN_GROUP, ATTN_GROUP), np.float32)
    for r in range(d):
        for m in range(g):
            p[r * g + m, d * m + r] = 1.0
    return jnp.asarray(p, BF16), jnp.asarray(p.T, BF16)


def _lse_expand_table():
    e = np.zeros((ATTN_PAIRS, 2 * ATTN_BLK, 2 * ATTN_DH), np.float32)
    for hp in range(ATTN_PAIRS):
        for dst in range(2 * ATTN_DH):
            src = (dst // ATTN_DH) * ATTN_DH + 16 * hp
            e[hp, src, dst] = 1.0
            e[hp, ATTN_BLK + src, dst] = 1.0
    return jnp.asarray(e, BF16)


def _attn_consts():
    blk = ATTN_BLK
    qi = lax.broadcasted_iota(jnp.int32, (blk, 2 * blk), 0)
    kj = lax.broadcasted_iota(jnp.int32, (blk, 2 * blk), 1)
    band = (kj >= qi) & (kj <= qi + blk)
    neg = jnp.float32(-jnp.inf)
    bias_mid = jnp.where(band, 0.0, neg)
    bias_first = jnp.where(band & (kj >= blk), 0.0, neg)
    lane = lax.broadcasted_iota(jnp.int32, (blk, 2 * ATTN_DH), 1)
    return bias_mid, bias_first, lane


def _attn_scores(load_q, kcats, bias, low):
    zero = jnp.zeros((), BF16)
    out = []
    for hp in range(ATTN_PAIRS):
        q2 = load_q(hp)
        for half in range(2):
            qh = jnp.where(low if half == 0 else ~low, q2, zero)
            out.append(lax.dot_general(qh, kcats[hp], NT_DIMS, preferred_element_type=F32) + bias)
    return out


def _attn_softmax_pv(scores, vcats, low):
    one = jnp.ones((), BF16)
    lowv = lax.broadcasted_iota(jnp.int32, (2 * ATTN_BLK, 2 * ATTN_DH), 1) < ATTN_DH
    maxes = [jnp.max(s, axis=-1, keepdims=True) for s in scores]
    probs = [jnp.exp(s - m).astype(BF16) for s, m in zip(scores, maxes)]
    res = []
    for hp in range(ATTN_PAIRS):
        ra = _dot(probs[2 * hp], jnp.where(lowv, vcats[hp], one))
        rb = _dot(probs[2 * hp + 1], jnp.where(lowv, one, vcats[hp]))
        den = pltpu.roll(jnp.where(low, rb, ra), ATTN_DH, axis=1)
        out = jnp.where(low, ra, rb) / den
        lse = jnp.where(low, maxes[2 * hp], maxes[2 * hp + 1]) + jnp.log(den)
        res.append((out, lse))
    return res


def _attn_strided_kernel(d, q_ref, k_ref, v_ref, perm_ref, permt_ref, o_ref, lsec_ref,
                         qs, ks, vs, kp, vp, ost, lst):
    blk = ATTN_BLK
    grp = ATTN_GROUP // d
    rrows = ATTN_SUPER // d
    nj = rrows // blk
    ngroups = ATTN_SUPER // ATTN_GROUP
    step = pl.program_id(1)

    @pl.when(step == 0)
    def _():
        kp[...] = jnp.zeros_like(kp)
        vp[...] = jnp.zeros_like(vp)

    perm = perm_ref[...]

    def reorder(src, dst):
        for g in range(ngroups):
            y = _dot(perm, src[g * ATTN_GROUP:(g + 1) * ATTN_GROUP, :]).astype(BF16)
            for r in range(d):
                dst[r, g * grp:(g + 1) * grp, :] = y[r * grp:(r + 1) * grp, :]

    reorder(q_ref, qs)
    reorder(k_ref, ks)

    bias_mid, bias_first, lane = _attn_consts()
    low = lane < ATTN_DH
    lane_grp = (lane % ATTN_DH) // 16
    units = [(r, j) for r in range(d) for j in range(nj)]

    def cat(cur, prev, r, j, hp):
        cols = slice(hp * 2 * ATTN_DH, (hp + 1) * 2 * ATTN_DH)
        before = prev[r, :, cols] if j == 0 else cur[r, (j - 1) * blk:j * blk, cols]
        return jnp.concatenate([before, cur[r, j * blk:(j + 1) * blk, cols]], axis=0)

    def scores(u):
        r, j = units[u]
        bias = jnp.where(step == 0, bias_first, bias_mid) if j == 0 else bias_mid
        kcats = [cat(ks, kp, r, j, hp) for hp in range(ATTN_PAIRS)]
        load_q = lambda hp: qs[r, j * blk:(j + 1) * blk, hp * 2 * ATTN_DH:(hp + 1) * 2 * ATTN_DH]
        return _attn_scores(load_q, kcats, bias, low)

    def finish(u, sc):
        r, j = units[u]
        rows = slice(j * blk, (j + 1) * blk)
        res = _attn_softmax_pv(sc, [cat(vs, vp, r, j, hp) for hp in range(ATTN_PAIRS)], low)
        comp = res[ATTN_PAIRS - 1][1]
        for hp in range(ATTN_PAIRS - 2, -1, -1):
            comp = jnp.where(lane_grp == hp, res[hp][1], comp)
        for hp in range(ATTN_PAIRS):
            ost[r, rows, hp * 2 * ATTN_DH:(hp + 1) * 2 * ATTN_DH] = res[hp][0].astype(BF16)
        hi = comp.astype(BF16)
        lst[r, rows, 0:blk] = hi
        lst[r, rows, blk:2 * blk] = (comp - hi.astype(F32)).astype(BF16)

    nxt = scores(0)
    reorder(v_ref, vs)
    for u in range(len(units)):
        cur = nxt
        if u + 1 < len(units):
            nxt = scores(u + 1)
        finish(u, cur)

    permt = permt_ref[...]
    for g in range(ngroups):
        rows = slice(g * ATTN_GROUP, (g + 1) * ATTN_GROUP)
        og = jnp.concatenate([ost[r, g * grp:(g + 1) * grp, :] for r in range(d)], axis=0)
        o_ref[rows, :] = _dot(permt, og).astype(BF16)
        lg = jnp.concatenate([lst[r, g * grp:(g + 1) * grp, :] for r in range(d)], axis=0)
        lsec_ref[rows, :] = _dot(permt, lg).astype(BF16)

    kp[...] = ks[:, rrows - blk:rrows, :]
    vp[...] = vs[:, rrows - blk:rrows, :]


def _attn_merge_kernel(q_ref, k_ref, kp_ref, v_ref, vp_ref, o16_ref, l16_ref, o4_ref, l4_ref,
                       ee_ref, o_ref):
    blk = ATTN_BLK
    nq = ATTN_D1_ROWS // blk
    step = pl.program_id(1)
    bias_mid, bias_first, lane = _attn_consts()
    low = lane < ATTN_DH

    def cat(cur, prev, j, hp):
        cols = slice(hp * 2 * ATTN_DH, (hp + 1) * 2 * ATTN_DH)
        before = prev[:, cols] if j == 0 else cur[(j - 1) * blk:j * blk, cols]
        return jnp.concatenate([before, cur[j * blk:(j + 1) * blk, cols]], axis=0)

    def scores(j):
        bias = jnp.where(step == 0, bias_first, bias_mid) if j == 0 else bias_mid
        kcats = [cat(k_ref, kp_ref, j, hp) for hp in range(ATTN_PAIRS)]
        load_q = lambda hp: q_ref[j * blk:(j + 1) * blk, hp * 2 * ATTN_DH:(hp + 1) * 2 * ATTN_DH]
        return _attn_scores(load_q, kcats, bias, low)

    def finish(j, sc):
        rows = slice(j * blk, (j + 1) * blk)
        res = _attn_softmax_pv(sc, [cat(v_ref, vp_ref, j, hp) for hp in range(ATTN_PAIRS)], low)
        l16c = l16_ref[rows, :]
        l4c = l4_ref[rows, :]
        for hp in range(ATTN_PAIRS):
            cols = slice(hp * 2 * ATTN_DH, (hp + 1) * 2 * ATTN_DH)
            o1, lse1 = res[hp]
            lse16 = _dot(l16c, ee_ref[hp])
            lse4 = _dot(l4c, ee_ref[hp])
            mx = jnp.maximum(jnp.maximum(lse16, lse4), lse1)
            w16 = jnp.exp(lse16 - mx)
            w4 = jnp.exp(lse4 - mx)
            w1 = jnp.exp(lse1 - mx)
            num = (w16 * o16_ref[rows, cols].astype(F32) + w4 * o4_ref[rows, cols].astype(F32)
                   + w1 * o1)
            o_ref[rows, cols] = (num / (w16 + w4 + w1)).astype(o_ref.dtype)

    nxt = scores(0)
    for j in range(nq):
        cur = nxt
        if j + 1 < nq:
            nxt = scores(j + 1)
        finish(j, cur)


def _dilated_attention(q, k, v):
    b, s, _ = q.shape
    blk = ATTN_BLK
    sds = jax.ShapeDtypeStruct
    params = pltpu.CompilerParams(dimension_semantics=("arbitrary", "arbitrary"),
                                  vmem_limit_bytes=VMEM_LIMIT)
    partial_outs = []
    for d in (16, 4):
        perm, permt = _attn_tables(d)
        rrows = ATTN_SUPER // d
        tokens = lambda w: pl.BlockSpec((None, ATTN_SUPER, w), lambda bi, i: (bi, i, 0))
        table = pl.BlockSpec((ATTN_GROUP, ATTN_GROUP), lambda bi, i: (0, 0))
        residue_major = lambda rows, w: pltpu.VMEM((d, rows, w), BF16)
        partial_outs += pl.pallas_call(
            functools.partial(_attn_strided_kernel, d),
            grid=(b, s // ATTN_SUPER),
            in_specs=[tokens(ATTN_WIDTH)] * 3 + [table, table],
            out_specs=[tokens(ATTN_WIDTH), tokens(2 * blk)],
            out_shape=[sds((b, s, ATTN_WIDTH), BF16), sds((b, s, 2 * blk), BF16)],
            scratch_shapes=[residue_major(rrows, ATTN_WIDTH)] * 3
                           + [residue_major(blk, ATTN_WIDTH)] * 2
                           + [residue_major(rrows, ATTN_WIDTH), residue_major(rrows, 2 * blk)],
            compiler_params=params,
            name=f"dilated_attn_d{d}",
        )(q, k, v, perm, permt)

    nq = ATTN_D1_ROWS // blk
    tokens = lambda w: pl.BlockSpec((None, ATTN_D1_ROWS, w), lambda bi, i: (bi, i, 0))
    prev = pl.BlockSpec((None, blk, ATTN_WIDTH), lambda bi, i: (bi, jnp.maximum(i * nq - 1, 0), 0))
    ee = _lse_expand_table()
    return pl.pallas_call(
        _attn_merge_kernel,
        grid=(b, s // ATTN_D1_ROWS),
        in_specs=[tokens(ATTN_WIDTH), tokens(ATTN_WIDTH), prev, tokens(ATTN_WIDTH), prev,
                  tokens(ATTN_WIDTH), tokens(2 * blk), tokens(ATTN_WIDTH), tokens(2 * blk),
                  pl.BlockSpec(ee.shape, lambda bi, i: (0, 0, 0))],
        out_specs=tokens(ATTN_WIDTH),
        out_shape=sds((b, s, ATTN_WIDTH), BF16),
        compiler_params=params,
        name="dilated_attn_d1_merge",
    )(q, k, k, v, v, *partial_outs, ee)


def _layer_norm(y, g, b):
    mu = jnp.mean(y, axis=-1, keepdims=True)
    yc = y - mu
    var = jnp.mean(yc * yc, axis=-1, keepdims=True)
    return yc * lax.rsqrt(var + LN_EPS) * g + b


def _out_mlp_kernel(x_ref, orec_ref, oatt_ref, wout_ref, g1_ref, b1_ref, w1_ref, w2_ref,
                    g2_ref, b2_ref, y_ref):
    sub = MLP_SUBTILE
    tiles = [slice(r, r + sub) for r in range(0, x_ref.shape[0], sub)]
    mixes = [_dot(orec_ref[rows, :], wout_ref[0:HGRN_WIDTH, :])
             + _dot(oatt_ref[rows, :], wout_ref[HGRN_WIDTH:D_MODEL, :]) for rows in tiles]
    for rows, mix in zip(tiles, mixes):
        x1 = _layer_norm(DEEPNORM_ALPHA * x_ref[rows, :] + mix, g1_ref[...], b1_ref[...])
        x1b = x1.astype(BF16)
        acc = jnp.zeros(x1.shape, F32)
        for c in range(D_FF // MLP_FF_CHUNK):
            cols = slice(c * MLP_FF_CHUNK, (c + 1) * MLP_FF_CHUNK)
            h = jnp.maximum(_dot(x1b, w1_ref[:, cols]), 0.0)
            acc = acc + _dot((h * h).astype(BF16), w2_ref[cols, :])
        y_ref[rows, :] = _layer_norm(DEEPNORM_ALPHA * x1 + acc, g2_ref[...], b2_ref[...])


def _out_mlp(x2d, o_rec, o_att, w_out, g1, b1, w1, w2, g2, b2):
    t = x2d.shape[0]
    tm = MLP_TILE
    row = lambda i: (i, 0)
    const = lambda shape: pl.BlockSpec(shape, lambda i: (0, 0), pipeline_mode=pl.Buffered(1))
    return pl.pallas_call(
        _out_mlp_kernel,
        grid=(t // tm,),
        in_specs=[pl.BlockSpec((tm, D_MODEL), row),
                  pl.BlockSpec((tm, HGRN_WIDTH), row),
                  pl.BlockSpec((tm, ATTN_WIDTH), row),
                  const((D_MODEL, D_MODEL)), const((1, D_MODEL)), const((1, D_MODEL)),
                  const((D_MODEL, D_FF)), const((D_FF, D_MODEL)),
                  const((1, D_MODEL)), const((1, D_MODEL))],
        out_specs=pl.BlockSpec((tm, D_MODEL), row),
        out_shape=jax.ShapeDtypeStruct((t, D_MODEL), F32),
        compiler_params=pltpu.CompilerParams(dimension_semantics=("arbitrary",),
                                             vmem_limit_bytes=MLP_VMEM_LIMIT),
        name="out_proj_mlp",
    )(x2d, o_rec, o_att, w_out, g1, b1, w1, w2, g2, b2)


def kernel(x, w_in, w_out, lower_bounds, hgrn_norm_w, ln1_g, ln1_b, w_ff1, w_ff2, ln2_g, ln2_b):
    b, s, _ = x.shape
    t = b * s
    x2d = x.reshape(t, D_MODEL)
    row = lambda a, l: a[l].reshape(1, -1).astype(F32)
    for l in range(DEPTH):
        lbraw = lower_bounds.astype(F32)
        hg, hk, q, k, v = _in_proj(x2d, w_in[l].astype(BF16), lbraw, l)
        o_rec = _hgrn(hg.reshape(b, s, 4 * HGRN_WIDTH), hk.reshape(b, s, HGRN_WIDTH), lbraw,
                      row(hgrn_norm_w, l), l)
        o_att = _dilated_attention(*(a.reshape(b, s, ATTN_WIDTH) for a in (q, k, v)))
        x2d = _out_mlp(x2d, o_rec.reshape(t, HGRN_WIDTH), o_att.reshape(t, ATTN_WIDTH),
                       w_out[l].astype(BF16), row(ln1_g, l), row(ln1_b, l),
                       w_ff1[l].astype(BF16), w_ff2[l].astype(BF16), row(ln2_g, l), row(ln2_b, l))
    return x2d.reshape(b, s, D_MODEL)
```

```python
import functools

import numpy as np
import jax
import jax.numpy as jnp
from jax import lax
from jax.experimental import pallas as pl
from jax.experimental.pallas import tpu as pltpu

F32 = jnp.float32
BF16 = jnp.bfloat16

D_MODEL = 1024
HGRN_WIDTH = 512
HGRN_HEADS = 4
HGRN_DK = 128
ATTN_WIDTH = 512
ATTN_HEADS = 8
ATTN_DH = 64
ATTN_PAIRS = ATTN_HEADS // 2
ATTN_BLK = 128
ATTN_SUPER = 2048
ATTN_GROUP = 256
ATTN_D1_ROWS = 1024
D_FF = 4 * D_MODEL
IN_COLS = 4 * HGRN_WIDTH + 3 * ATTN_WIDTH
DEPTH = 2
DEEPNORM_ALPHA = (2 * DEPTH) ** 0.25
LN_EPS = 1e-5
RMS_EPS = 1e-6
LOG2_E = 1.4426950408889634

HGRN_CHUNK = 64
HGRN_LEVELS = 6
HGRN_GROUP_CHUNKS = 4
HGRN_ROWS = 1024
TOKEN_TILE = 512
MLP_TILE = 1024
MLP_SUBTILE = 256
MLP_FF_CHUNK = 1024
VMEM_LIMIT = 48 * 1024 * 1024
MLP_VMEM_LIMIT = 56 * 1024 * 1024

NT_DIMS = (((1,), (1,)), ((), ()))
TN_DIMS = (((0,), (0,)), ((), ()))


def _dot(a, b):
    return jnp.dot(a, b, preferred_element_type=F32)


def _sigmoid(x):
    return 0.5 * jnp.tanh(0.5 * x) + 0.5


def _hgrn_lower_bound(lbraw_ref, layer):
    raw = lbraw_ref[...]
    e = jnp.exp(raw - jnp.max(raw, axis=0, keepdims=True))
    sm = e / jnp.sum(e, axis=0, keepdims=True)
    first = sm[0:1, :]
    cum = first
    for j in range(1, layer + 1):
        cum = cum + sm[j:j + 1, :]
    return cum - first


def _in_proj_kernel(layer, x_ref, wf_ref, lbraw_ref, hg_ref, hk_ref, q_ref, k_ref, v_ref, w_ref):
    @pl.when(pl.program_id(0) == 0)
    def _():
        w_ref[...] = wf_ref[...].astype(BF16)

    xb = x_ref[...].astype(BF16)
    part = lambda c: slice(c * HGRN_WIDTH, (c + 1) * HGRN_WIDTH)
    hq = _dot(xb, w_ref[:, part(0)])
    hg_ref[:, part(0)] = hq * _sigmoid(hq)
    z = _dot(xb, w_ref[:, part(1)])
    hg_ref[:, part(1)] = z
    lb = _hgrn_lower_bound(lbraw_ref, layer)
    hk_ref[...] = (1.0 - lb) * _sigmoid(-z)
    hg_ref[:, part(2)] = _dot(xb, w_ref[:, part(2)])
    hg_ref[:, part(3)] = _sigmoid(_dot(xb, w_ref[:, part(3)]))
    base = 4 * HGRN_WIDTH
    q = _dot(xb, w_ref[:, base:base + ATTN_WIDTH]) * (ATTN_DH ** -0.5)
    q_ref[...] = q.astype(BF16)
    k_ref[...] = _dot(xb, w_ref[:, base + ATTN_WIDTH:base + 2 * ATTN_WIDTH]).astype(BF16)
    v_ref[...] = _dot(xb, w_ref[:, base + 2 * ATTN_WIDTH:base + 3 * ATTN_WIDTH]).astype(BF16)


def _in_proj(x2d, w_in, lower_bounds, layer):
    t = x2d.shape[0]
    tm = TOKEN_TILE
    row = lambda i: (i, 0)
    return pl.pallas_call(
        functools.partial(_in_proj_kernel, layer),
        grid=(t // tm,),
        in_specs=[pl.BlockSpec((tm, D_MODEL), row),
                  pl.BlockSpec((D_MODEL, IN_COLS), lambda i: (0, 0), pipeline_mode=pl.Buffered(1)),
                  pl.BlockSpec((DEPTH, HGRN_WIDTH), lambda i: (0, 0))],
        out_specs=[pl.BlockSpec((tm, 4 * HGRN_WIDTH), row),
                   pl.BlockSpec((tm, HGRN_WIDTH), row),
                   pl.BlockSpec((tm, ATTN_WIDTH), row),
                   pl.BlockSpec((tm, ATTN_WIDTH), row),
                   pl.BlockSpec((tm, ATTN_WIDTH), row)],
        out_shape=[jax.ShapeDtypeStruct((t, 4 * HGRN_WIDTH), F32),
                   jax.ShapeDtypeStruct((t, HGRN_WIDTH), F32),
                   jax.ShapeDtypeStruct((t, ATTN_WIDTH), BF16),
                   jax.ShapeDtypeStruct((t, ATTN_WIDTH), BF16),
                   jax.ShapeDtypeStruct((t, ATTN_WIDTH), BF16)],
        scratch_shapes=[pltpu.VMEM((D_MODEL, IN_COLS), BF16)],
        compiler_params=pltpu.CompilerParams(dimension_semantics=("arbitrary",),
                                             vmem_limit_bytes=VMEM_LIMIT),
        name="in_proj",
    )(x2d, w_in, lower_bounds)


def _hgrn_tables():
    c, nl = HGRN_CHUNK, HGRN_LEVELS
    t = np.arange(c)[:, None]
    u = np.arange(c)[None, :]
    tabs = []
    for l in range(nl):
        h = 1 << l
        upper = ((t >> l) & 1) == 1
        r = ((t >> (l + 1)) << (l + 1)) + h - 1
        tabs.append(np.where(upper, (u > r) & (u <= t), (u > t) & (u <= r)))
    tabs.append(u <= t)
    tabs.append(u > t)
    w = np.concatenate(tabs, axis=0).astype(np.float32)
    w3 = np.concatenate([w, w, w], axis=1)
    x = t ^ u
    lvl = np.where(u < t, np.floor(np.log2(np.maximum(x, 1))), np.where(u == t, nl, nl + 1))
    return jnp.asarray(w3, BF16), jnp.asarray(lvl, jnp.int32)


def _hgrn_kernel(layer, q_ref, f_ref, i_ref, g_ref, k_ref, lbraw_ref, nw_ref, w3_ref, lvl_ref,
                 o_ref, state_ref):
    c, nl, nh, dk = HGRN_CHUNK, HGRN_LEVELS, HGRN_HEADS, HGRN_DK
    head = lambda a, h: a[:, h * dk:(h + 1) * dk]

    @pl.when(pl.program_id(1) == 0)
    def _():
        state_ref[...] = jnp.zeros_like(state_ref)

    lb = _hgrn_lower_bound(lbraw_ref, layer)
    log_lb = jnp.log(lb)
    log_1mlb = jnp.log1p(-lb)

    nw = nw_ref[...]
    w3 = w3_ref[...]
    lvl = lvl_ref[...]
    row = lax.broadcasted_iota(jnp.int32, (c, HGRN_WIDTH), 0)
    upper = [((row >> l) & 1) == 1 for l in range(nl)]

    def prepare(it):
        chunks = []
        for ch in range(HGRN_GROUP_CHUNKS):
            start = (it * HGRN_GROUP_CHUNKS + ch) * c
            rows = slice(start, start + c)
            q = q_ref[rows, :]
            k = k_ref[rows, :]
            z = f_ref[rows, :]
            v = i_ref[rows, :].astype(BF16)
            log_sig = jnp.minimum(z, 0.0) - jnp.log(1.0 + jnp.exp(-jnp.abs(z)))
            a = log_1mlb + log_sig
            lf = jnp.maximum(log_lb, a) + jnp.log(1.0 + jnp.exp(-jnp.abs(log_lb - a)))
            lf2 = lf * LOG2_E
            hi = lf2.astype(BF16)
            r1 = lf2 - hi.astype(F32)
            mid = r1.astype(BF16)
            lo = (r1 - mid.astype(F32)).astype(BF16)
            ex = jnp.exp2(_dot(w3, jnp.concatenate([hi, mid, lo], axis=0)))
            chunks.append((rows, q, k, v, ex))
        return chunks

    def process(chunks):
        amats = []
        for rows, q, k, v, ex in chunks:
            xs = [(jnp.where(upper[l], q, k) * ex[l * c:(l + 1) * c]).astype(BF16) for l in range(nl)]
            qb = q.astype(BF16)
            kb = k.astype(BF16)
            per_head = []
            for h in range(nh):
                amat = lax.dot_general(head(qb, h), head(kb, h), NT_DIMS, preferred_element_type=F32)
                amat = jnp.where(lvl == nl, amat, 0.0)
                for l in range(nl):
                    x = head(xs[l], h)
                    s = lax.dot_general(x, x, NT_DIMS, preferred_element_type=F32)
                    amat = jnp.where(lvl == l, s, amat)
                per_head.append(amat.astype(BF16))
            amats.append(per_head)

        upds = []
        for rows, q, k, v, ex in chunks:
            kd = (k * ex[(nl + 1) * c:(nl + 2) * c]).astype(BF16)
            upds.append([lax.dot_general(head(v, h), head(kd, h), TN_DIMS, preferred_element_type=F32)
                         for h in range(nh)])

        before = []
        for h in range(nh):
            st = state_ref[h]
            seen = []
            for ci, (rows, q, k, v, ex) in enumerate(chunks):
                seen.append(st.astype(BF16))
                st = st * head(ex[(nl + 1) * c - 1:(nl + 1) * c], h) + upds[ci][h]
            state_ref[h] = st
            before.append(seen)

        for ci, (rows, q, k, v, ex) in enumerate(chunks):
            qs = (q * ex[nl * c:(nl + 1) * c]).astype(BF16)
            outs = []
            for h in range(nh):
                o = _dot(amats[ci][h], head(v, h))
                o = o + lax.dot_general(head(qs, h), before[h][ci], NT_DIMS, preferred_element_type=F32)
                outs.append(o * lax.rsqrt(jnp.mean(o * o, axis=-1, keepdims=True) + RMS_EPS))
            o = jnp.concatenate(outs, axis=1) * nw
            o_ref[rows, :] = (o * g_ref[rows, :]).astype(o_ref.dtype)

    ngroups = q_ref.shape[0] // (c * HGRN_GROUP_CHUNKS)
    nxt = prepare(0)
    for it in range(ngroups):
        cur = nxt
        if it + 1 < ngroups:
            nxt = prepare(it + 1)
        process(cur)


def _hgrn(hg3d, hk3d, lower_bounds, norm_w_row, layer):
    b, s, _ = hg3d.shape
    rows = HGRN_ROWS
    w3, lvl = _hgrn_tables()
    part = lambda p: pl.BlockSpec((None, rows, HGRN_WIDTH), lambda bi, i: (bi, i, p))
    full = lambda shape: pl.BlockSpec(shape, lambda bi, i: (0, 0))
    return pl.pallas_call(
        functools.partial(_hgrn_kernel, layer),
        grid=(b, s // rows),
        in_specs=[part(0), part(1), part(2), part(3), part(0),
                  full((DEPTH, HGRN_WIDTH)), full((1, HGRN_WIDTH)), full(w3.shape), full(lvl.shape)],
        out_specs=pl.BlockSpec((None, rows, HGRN_WIDTH), lambda bi, i: (bi, i, 0)),
        out_shape=jax.ShapeDtypeStruct((b, s, HGRN_WIDTH), BF16),
        scratch_shapes=[pltpu.VMEM((HGRN_HEADS, HGRN_DK, HGRN_DK), F32)],
        compiler_params=pltpu.CompilerParams(dimension_semantics=("arbitrary", "arbitrary"),
                                             vmem_limit_bytes=VMEM_LIMIT),
        name="hgrn2",
    )(hg3d, hg3d, hg3d, hg3d, hk3d, lower_bounds, norm_w_row, w3, lvl)


def _attn_tables(d):
    g = ATTN_GROUP // d
    p = np.zeros((ATTN_GROUP, ATTN_GROUP), np.float32)
    for r in range(d):
        for m in range(g):
            p[r * g + m, d * m + r] = 1.0
    return jnp.asarray(p, BF16), jnp.asarray(p.T, BF16)


def _lse_expand_table():
    e = np.zeros((ATTN_PAIRS, 2 * ATTN_BLK, 2 * ATTN_DH), np.float32)
    for hp in range(ATTN_PAIRS):
        for dst in range(2 * ATTN_DH):
            src = (dst // ATTN_DH) * ATTN_DH + 16 * hp
            e[hp, src, dst] = 1.0
            e[hp, ATTN_BLK + src, dst] = 1.0
    return jnp.asarray(e, BF16)


def _attn_consts():
    blk = ATTN_BLK
    qi = lax.broadcasted_iota(jnp.int32, (blk, 2 * blk), 0)
    kj = lax.broadcasted_iota(jnp.int32, (blk, 2 * blk), 1)
    band = (kj >= qi) & (kj <= qi + blk)
    neg = jnp.float32(-jnp.inf)
    bias_mid = jnp.where(band, 0.0, neg)
    bias_first = jnp.where(band & (kj >= blk), 0.0, neg)
    lane = lax.broadcasted_iota(jnp.int32, (blk, 2 * ATTN_DH), 1)
    return bias_mid, bias_first, lane


def _attn_scores(load_q, kcats, bias, low):
    zero = jnp.zeros((), BF16)
    out = []
    for hp in range(ATTN_PAIRS):
        q2 = load_q(hp)
        for half in range(2):
            qh = jnp.where(low if half == 0 else ~low, q2, zero)
            out.append(lax.dot_general(qh, kcats[hp], NT_DIMS, preferred_element_type=F32) + bias)
    return out


def _attn_softmax_pv(scores, vcats, low):
    one = jnp.ones((), BF16)
    lowv = lax.broadcasted_iota(jnp.int32, (2 * ATTN_BLK, 2 * ATTN_DH), 1) < ATTN_DH
    maxes = [jnp.max(s, axis=-1, keepdims=True) for s in scores]
    probs = [jnp.exp(s - m).astype(BF16) for s, m in zip(scores, maxes)]
    res = []
    for hp in range(ATTN_PAIRS):
        ra = _dot(probs[2 * hp], jnp.where(lowv, vcats[hp], one))
        rb = _dot(probs[2 * hp + 1], jnp.where(lowv, one, vcats[hp]))
        den = pltpu.roll(jnp.where(low, rb, ra), ATTN_DH, axis=1)
        out = jnp.where(low, ra, rb) / den
        lse = jnp.where(low, maxes[2 * hp], maxes[2 * hp + 1]) + jnp.log(den)
        res.append((out, lse))
    return res


def _attn_strided_kernel(d, q_ref, k_ref, v_ref, perm_ref, permt_ref, o_ref, lsec_ref,
                         qs, ks, vs, kp, vp, ost, lst):
    blk = ATTN_BLK
    grp = ATTN_GROUP // d
    rrows = ATTN_SUPER // d
    nj = rrows // blk
    ngroups = ATTN_SUPER // ATTN_GROUP
    step = pl.program_id(1)

    @pl.when(step == 0)
    def _():
        kp[...] = jnp.zeros_like(kp)
        vp[...] = jnp.zeros_like(vp)

    perm = perm_ref[...]

    def reorder(src, dst):
        for g in range(ngroups):
            y = _dot(perm, src[g * ATTN_GROUP:(g + 1) * ATTN_GROUP, :]).astype(BF16)
            for r in range(d):
                dst[r, g * grp:(g + 1) * grp, :] = y[r * grp:(r + 1) * grp, :]

    reorder(q_ref, qs)
    reorder(k_ref, ks)

    bias_mid, bias_first, lane = _attn_consts()
    low = lane < ATTN_DH
    lane_grp = (lane % ATTN_DH) // 16
    units = [(r, j) for r in range(d) for j in range(nj)]

    def cat(cur, prev, r, j, hp):
        cols = slice(hp * 2 * ATTN_DH, (hp + 1) * 2 * ATTN_DH)
        before = prev[r, :, cols] if j == 0 else cur[r, (j - 1) * blk:j * blk, cols]
        return jnp.concatenate([before, cur[r, j * blk:(j + 1) * blk, cols]], axis=0)

    def scores(u):
        r, j = units[u]
        bias = jnp.where(step == 0, bias_first, bias_mid) if j == 0 else bias_mid
        kcats = [cat(ks, kp, r, j, hp) for hp in range(ATTN_PAIRS)]
        load_q = lambda hp: qs[r, j * blk:(j + 1) * blk, hp * 2 * ATTN_DH:(hp + 1) * 2 * ATTN_DH]
        return _attn_scores(load_q, kcats, bias, low)

    def finish(u, sc):
        r, j = units[u]
        rows = slice(j * blk, (j + 1) * blk)
        res = _attn_softmax_pv(sc, [cat(vs, vp, r, j, hp) for hp in range(ATTN_PAIRS)], low)
        comp = res[ATTN_PAIRS - 1][1]
        for hp in range(ATTN_PAIRS - 2, -1, -1):
            comp = jnp.where(lane_grp == hp, res[hp][1], comp)
        for hp in range(ATTN_PAIRS):
            ost[r, rows, hp * 2 * ATTN_DH:(hp + 1) * 2 * ATTN_DH] = res[hp][0].astype(BF16)
        hi = comp.astype(BF16)
        lst[r, rows, 0:blk] = hi
        lst[r, rows, blk:2 * blk] = (comp - hi.astype(F32)).astype(BF16)

    nxt = scores(0)
    reorder(v_ref, vs)
    for u in range(len(units)):
        cur = nxt
        if u + 1 < len(units):
            nxt = scores(u + 1)
        finish(u, cur)

    permt = permt_ref[...]
    for g in range(ngroups):
        rows = slice(g * ATTN_GROUP, (g + 1) * ATTN_GROUP)
        og = jnp.concatenate([ost[r, g * grp:(g + 1) * grp, :] for r in range(d)], axis=0)
        o_ref[rows, :] = _dot(permt, og).astype(BF16)
        lg = jnp.concatenate([lst[r, g * grp:(g + 1) * grp, :] for r in range(d)], axis=0)
        lsec_ref[rows, :] = _dot(permt, lg).astype(BF16)

    kp[...] = ks[:, rrows - blk:rrows, :]
    vp[...] = vs[:, rrows - blk:rrows, :]


def _attn_merge_kernel(q_ref, k_ref, kp_ref, v_ref, vp_ref, o16_ref, l16_ref, o4_ref, l4_ref,
                       ee_ref, o_ref):
    blk = ATTN_BLK
    nq = ATTN_D1_ROWS // blk
    step = pl.program_id(1)
    bias_mid, bias_first, lane = _attn_consts()
    low = lane < ATTN_DH

    def cat(cur, prev, j, hp):
        cols = slice(hp * 2 * ATTN_DH, (hp + 1) * 2 * ATTN_DH)
        before = prev[:, cols] if j == 0 else cur[(j - 1) * blk:j * blk, cols]
        return jnp.concatenate([before, cur[j * blk:(j + 1) * blk, cols]], axis=0)

    def scores(j):
        bias = jnp.where(step == 0, bias_first, bias_mid) if j == 0 else bias_mid
        kcats = [cat(k_ref, kp_ref, j, hp) for hp in range(ATTN_PAIRS)]
        load_q = lambda hp: q_ref[j * blk:(j + 1) * blk, hp * 2 * ATTN_DH:(hp + 1) * 2 * ATTN_DH]
        return _attn_scores(load_q, kcats, bias, low)

    def finish(j, sc):
        rows = slice(j * blk, (j + 1) * blk)
        res = _attn_softmax_pv(sc, [cat(v_ref, vp_ref, j, hp) for hp in range(ATTN_PAIRS)], low)
        l16c = l16_ref[rows, :]
        l4c = l4_ref[rows, :]
        for hp in range(ATTN_PAIRS):
            cols = slice(hp * 2 * ATTN_DH, (hp + 1) * 2 * ATTN_DH)
            o1, lse1 = res[hp]
            lse16 = _dot(l16c, ee_ref[hp])
            lse4 = _dot(l4c, ee_ref[hp])
            mx = jnp.maximum(jnp.maximum(lse16, lse4), lse1)
            w16 = jnp.exp(lse16 - mx)
            w4 = jnp.exp(lse4 - mx)
            w1 = jnp.exp(lse1 - mx)
            num = (w16 * o16_ref[rows, cols].astype(F32) + w4 * o4_ref[rows, cols].astype(F32)
                   + w1 * o1)
            o_ref[rows, cols] = (num / (w16 + w4 + w1)).astype(o_ref.dtype)

    nxt = scores(0)
    for j in range(nq):
        cur = nxt
        if j + 1 < nq:
            nxt = scores(j + 1)
        finish(j, cur)


def _dilated_attention(q, k, v):
    b, s, _ = q.shape
    blk = ATTN_BLK
    sds = jax.ShapeDtypeStruct
    params = pltpu.CompilerParams(dimension_semantics=("arbitrary", "arbitrary"),
                                  vmem_limit_bytes=VMEM_LIMIT)
    partial_outs = []
    for d in (16, 4):
        perm, permt = _attn_tables(d)
        rrows = ATTN_SUPER // d
        tokens = lambda w: pl.BlockSpec((None, ATTN_SUPER, w), lambda bi, i: (bi, i, 0))
        table = pl.BlockSpec((ATTN_GROUP, ATTN_GROUP), lambda bi, i: (0, 0))
        residue_major = lambda rows, w: pltpu.VMEM((d, rows, w), BF16)
        partial_outs += pl.pallas_call(
            functools.partial(_attn_strided_kernel, d),
            grid=(b, s // ATTN_SUPER),
            in_specs=[tokens(ATTN_WIDTH)] * 3 + [table, table],
            out_specs=[tokens(ATTN_WIDTH), tokens(2 * blk)],
            out_shape=[sds((b, s, ATTN_WIDTH), BF16), sds((b, s, 2 * blk), BF16)],
            scratch_shapes=[residue_major(rrows, ATTN_WIDTH)] * 3
                           + [residue_major(blk, ATTN_WIDTH)] * 2
                           + [residue_major(rrows, ATTN_WIDTH), residue_major(rrows, 2 * blk)],
            compiler_params=params,
            name=f"dilated_attn_d{d}",
        )(q, k, v, perm, permt)

    nq = ATTN_D1_ROWS // blk
    tokens = lambda w: pl.BlockSpec((None, ATTN_D1_ROWS, w), lambda bi, i: (bi, i, 0))
    prev = pl.BlockSpec((None, blk, ATTN_WIDTH), lambda bi, i: (bi, jnp.maximum(i * nq - 1, 0), 0))
    ee = _lse_expand_table()
    return pl.pallas_call(
        _attn_merge_kernel,
        grid=(b, s // ATTN_D1_ROWS),
        in_specs=[tokens(ATTN_WIDTH), tokens(ATTN_WIDTH), prev, tokens(ATTN_WIDTH), prev,
                  tokens(ATTN_WIDTH), tokens(2 * blk), tokens(ATTN_WIDTH), tokens(2 * blk),
                  pl.BlockSpec(ee.shape, lambda bi, i: (0, 0, 0))],
        out_specs=tokens(ATTN_WIDTH),
        out_shape=sds((b, s, ATTN_WIDTH), BF16),
        compiler_params=params,
        name="dilated_attn_d1_merge",
    )(q, k, k, v, v, *partial_outs, ee)


def _layer_norm(y, g, b):
    mu = jnp.mean(y, axis=-1, keepdims=True)
    yc = y - mu
    var = jnp.mean(yc * yc, axis=-1, keepdims=True)
    return yc * lax.rsqrt(var + LN_EPS) * g + b


def _out_mlp_kernel(x_ref, orec_ref, oatt_ref, wout_ref, g1_ref, b1_ref, w1_ref, w2_ref,
                    g2_ref, b2_ref, y_ref):
    sub = MLP_SUBTILE
    tiles = [slice(r, r + sub) for r in range(0, x_ref.shape[0], sub)]
    mixes = [_dot(orec_ref[rows, :], wout_ref[0:HGRN_WIDTH, :])
             + _dot(oatt_ref[rows, :], wout_ref[HGRN_WIDTH:D_MODEL, :]) for rows in tiles]
    for rows, mix in zip(tiles, mixes):
        x1 = _layer_norm(DEEPNORM_ALPHA * x_ref[rows, :] + mix, g1_ref[...], b1_ref[...])
        x1b = x1.astype(BF16)
        acc = jnp.zeros(x1.shape, F32)
        for c in range(D_FF // MLP_FF_CHUNK):
            cols = slice(c * MLP_FF_CHUNK, (c + 1) * MLP_FF_CHUNK)
            h = jnp.maximum(_dot(x1b, w1_ref[:, cols]), 0.0)
            acc = acc + _dot((h * h).astype(BF16), w2_ref[cols, :])
        y_ref[rows, :] = _layer_norm(DEEPNORM_ALPHA * x1 + acc, g2_ref[...], b2_ref[...])


def _out_mlp(x2d, o_rec, o_att, w_out, g1, b1, w1, w2, g2, b2):
    t = x2d.shape[0]
    tm = MLP_TILE
    row = lambda i: (i, 0)
    const = lambda shape: pl.BlockSpec(shape, lambda i: (0, 0), pipeline_mode=pl.Buffered(1))
    return pl.pallas_call(
        _out_mlp_kernel,
        grid=(t // tm,),
        in_specs=[pl.BlockSpec((tm, D_MODEL), row),
                  pl.BlockSpec((tm, HGRN_WIDTH), row),
                  pl.BlockSpec((tm, ATTN_WIDTH), row),
                  const((D_MODEL, D_MODEL)), const((1, D_MODEL)), const((1, D_MODEL)),
                  const((D_MODEL, D_FF)), const((D_FF, D_MODEL)),
                  const((1, D_MODEL)), const((1, D_MODEL))],
        out_specs=pl.BlockSpec((tm, D_MODEL), row),
        out_shape=jax.ShapeDtypeStruct((t, D_MODEL), F32),
        compiler_params=pltpu.CompilerParams(dimension_semantics=("arbitrary",),
                                             vmem_limit_bytes=MLP_VMEM_LIMIT),
        name="out_proj_mlp",
    )(x2d, o_rec, o_att, w_out, g1, b1, w1, w2, g2, b2)


def kernel(x, w_in, w_out, lower_bounds, hgrn_norm_w, ln1_g, ln1_b, w_ff1, w_ff2, ln2_g, ln2_b):
    b, s, _ = x.shape
    t = b * s
    x2d = x.reshape(t, D_MODEL)
    row = lambda a, l: a[l].reshape(1, -1).astype(F32)
    for l in range(DEPTH):
        lbraw = lower_bounds.astype(F32)
        hg, hk, q, k, v = _in_proj(x2d, w_in[l].astype(F32), lbraw, l)
        o_rec = _hgrn(hg.reshape(b, s, 4 * HGRN_WIDTH), hk.reshape(b, s, HGRN_WIDTH), lbraw,
                      row(hgrn_norm_w, l), l)
        o_att = _dilated_attention(*(a.reshape(b, s, ATTN_WIDTH) for a in (q, k, v)))
        x2d = _out_mlp(x2d, o_rec.reshape(t, HGRN_WIDTH), o_att.reshape(t, ATTN_WIDTH),
                       w_out[l].astype(BF16), row(ln1_g, l), row(ln1_b, l),
                       w_ff1[l].astype(BF16), w_ff2[l].astype(BF16), row(ln2_g, l), row(ln2_b, l))
    return x2d.reshape(b, s, D_MODEL)
```
